```python
import math
import jax
import jax.numpy as jnp
from jax import lax
import numpy as np

D_MODEL = 1024
BATCH = 8
SEQ = 4096
DEPTH = 4

GRID_W = 64
CTX_LEN = 256
HEAD_DIM = 64
ROPE_THETA = 10000.0
BLK = 128
EPS = 1e-6
NEG = -1e30

A_HEADS = 8
A_KV = 2
A_GROUPS = A_HEADS // A_KV
WINDOW = 128
B_HEADS = 8
B_KV = 2
B_GROUPS = B_HEADS // B_KV
C_HEADS = 4
M_HEADS = 16
M_HEAD_DIM = 64
M_INNER = M_HEADS * M_HEAD_DIM
M_GROUPS = 2
M_HPG = M_HEADS // M_GROUPS
M_STATE = 128
M_CHUNK = 128
CONV_W = 3
M_XBC = M_INNER + 2 * M_GROUPS * M_STATE

N_BRANCH = 4
BR_WIDTHS = (A_HEADS * HEAD_DIM, B_HEADS * HEAD_DIM, 2 * C_HEADS * HEAD_DIM, M_INNER)
MIX_W = A_HEADS * HEAD_DIM + B_HEADS * HEAD_DIM + 2 * C_HEADS * HEAD_DIM + M_INNER

IN_PARTS = (
    ("a_q", A_HEADS * HEAD_DIM), ("a_k", A_KV * HEAD_DIM), ("a_v", A_KV * HEAD_DIM),
    ("b_q", B_HEADS * HEAD_DIM), ("b_k", B_KV * HEAD_DIM), ("b_v", B_KV * HEAD_DIM),
    ("c_q", 2 * C_HEADS * HEAD_DIM), ("c_k", 2 * C_HEADS * HEAD_DIM), ("c_v", 2 * C_HEADS * HEAD_DIM),
    ("d_z", M_INNER), ("d_xbc", M_XBC), ("d_dt", 2 * M_HEADS),
    ("g_a", D_MODEL), ("g_b", D_MODEL), ("g_c", D_MODEL), ("g_d", D_MODEL),
)
IN_W = (A_HEADS + 2 * A_KV + B_HEADS + 2 * B_KV + 6 * C_HEADS) * HEAD_DIM + 2 * M_INNER + 2 * M_GROUPS * M_STATE + 2 * M_HEADS + N_BRANCH * D_MODEL
GATE_NAMES = ("g_a", "g_b", "g_c", "g_d")
CTX_KV_PARTS = ("a_k", "a_v", "b_k", "b_v", "c_k", "c_v", "d_xbc", "d_dt")

P_HEADS = 8
N_KEYS = 128
N_EXP = N_KEYS * N_KEYS
P_DK = 256
P_TOPK = 16
P_CHUNK = 128

kernel_name = "hybrid_gated_mixers_peer_trunk"


def rmsnorm(x, g=None):
    xf = x.astype(jnp.float32)
    y = xf * lax.rsqrt(jnp.mean(xf * xf, axis=-1, keepdims=True) + EPS)
    if g is not None:
        y = y * g.astype(jnp.float32)
    return y.astype(x.dtype)


def modulate(h, shift, scale):
    return h * (1.0 + scale) + shift


def in_proj(h, w, names=None):
    out = {}
    off = 0
    for name, width in IN_PARTS:
        if names is None or name in names:
            out[name] = h @ w[:, off:off + width]
        off += width
    return out


def axial_rope_tables(rows):
    row = jnp.repeat(jnp.arange(rows, dtype=jnp.float32), GRID_W)
    col = jnp.tile(jnp.arange(GRID_W, dtype=jnp.float32), rows)
    nq = HEAD_DIM // 4
    inv = ROPE_THETA ** (-jnp.arange(nq, dtype=jnp.float32) / nq)
    ar = row[:, None] * inv
    ac = col[:, None] * inv
    ang = jnp.concatenate([ar, ar, ac, ac], axis=-1)
    return jnp.cos(ang), jnp.sin(ang)


def rope(x, cos, sin):
    x1, x2, x3, x4 = jnp.split(x, 4, axis=-1)
    rot = jnp.concatenate([-x2, x1, -x4, x3], axis=-1)
    return x * cos[:, None, :].astype(x.dtype) + rot * sin[:, None, :].astype(x.dtype)


def gqa_scores(q, k):
    return jnp.einsum("bqkgd,bskd->bkgqs", q, k).astype(jnp.float32) * (HEAD_DIM ** -0.5)


def gqa_values(p, v):
    return jnp.einsum("bkgqs,bskd->bqkgd", p.astype(v.dtype), v)


def softmax_with_sink(s, sink):
    m = jnp.maximum(jnp.max(s, axis=-1, keepdims=True), sink)
    e = jnp.exp(s - m)
    return e / (jnp.sum(e, axis=-1, keepdims=True) + jnp.exp(sink - m))


def sweep_query_blocks(block_fn, n_tok):
    out = lax.map(block_fn, jnp.arange(n_tok // BLK))
    nb, b, blk, w = out.shape
    return jnp.swapaxes(out, 0, 1).reshape(b, nb * blk, w)


def window_attention(pc, px, sink, cos, sin, need_ctx):
    b, s, _ = px["a_q"].shape
    lc = pc["a_k"].shape[1]
    q = rope(px["a_q"].reshape(b, s, A_HEADS, HEAD_DIM), cos, sin).reshape(b, s, A_KV, A_GROUPS, HEAD_DIM)
    k = rope(px["a_k"].reshape(b, s, A_KV, HEAD_DIM), cos, sin)
    v = px["a_v"].reshape(b, s, A_KV, HEAD_DIM)
    kc = pc["a_k"].reshape(b, lc, A_KV, HEAD_DIM)
    vc = pc["a_v"].reshape(b, lc, A_KV, HEAD_DIM)
    sink_b = sink.astype(jnp.float32).reshape(1, A_KV, A_GROUPS, 1, 1)
    pad = ((0, 0), (BLK, BLK), (0, 0), (0, 0))
    kp = jnp.pad(k, pad)
    vp = jnp.pad(v, pad)

    def block(n):
        start = n * BLK
        qb = lax.dynamic_slice_in_dim(q, start, BLK, axis=1)
        kb = lax.dynamic_slice_in_dim(kp, start, 3 * BLK, axis=1)
        vb = lax.dynamic_slice_in_dim(vp, start, 3 * BLK, axis=1)
        qpos = start + jnp.arange(BLK)
        kpos = start - BLK + jnp.arange(3 * BLK)
        valid = (jnp.abs(qpos[:, None] - kpos[None, :]) <= WINDOW) & (kpos[None, :] >= 0) & (kpos[None, :] < s)
        s_loc = jnp.where(valid, gqa_scores(qb, kb), NEG)
        p = softmax_with_sink(jnp.concatenate([gqa_scores(qb, kc), s_loc], axis=-1), sink_b)
        o = gqa_values(p[..., :lc], vc) + gqa_values(p[..., lc:], vb)
        return o.reshape(b, BLK, A_HEADS * HEAD_DIM)

    o_x = sweep_query_blocks(block, s)
    o_c = None
    if need_ctx:
        qc = pc["a_q"].reshape(b, lc, A_KV, A_GROUPS, HEAD_DIM)
        o_c = gqa_values(softmax_with_sink(gqa_scores(qc, kc), sink_b), vc).reshape(b, lc, A_HEADS * HEAD_DIM)
    return o_c, o_x


def qknorm_attention(pc, px, qn_g, kn_g, cos, sin, need_ctx):
    b, s, _ = px["b_q"].shape
    lc = pc["b_k"].shape[1]
    q = rope(rmsnorm(px["b_q"].reshape(b, s, B_HEADS, HEAD_DIM), qn_g), cos, sin).reshape(b, s, B_KV, B_GROUPS, HEAD_DIM)
    k = rope(rmsnorm(px["b_k"].reshape(b, s, B_KV, HEAD_DIM), kn_g), cos, sin)
    v = px["b_v"].reshape(b, s, B_KV, HEAD_DIM)
    kc = rmsnorm(pc["b_k"].reshape(b, lc, B_KV, HEAD_DIM), kn_g)
    vc = pc["b_v"].reshape(b, lc, B_KV, HEAD_DIM)
    k_all = jnp.concatenate([kc, k], axis=1)
    v_all = jnp.concatenate([vc, v], axis=1)

    def block(n):
        qb = lax.dynamic_slice_in_dim(q, n * BLK, BLK, axis=1)
        p = jax.nn.softmax(gqa_scores(qb, k_all), axis=-1)
        return gqa_values(p, v_all).reshape(b, BLK, B_HEADS * HEAD_DIM)

    o_x = sweep_query_blocks(block, s)
    o_c = None
    if need_ctx:
        qc = rmsnorm(pc["b_q"].reshape(b, lc, B_HEADS, HEAD_DIM), qn_g).reshape(b, lc, B_KV, B_GROUPS, HEAD_DIM)
        o_c = gqa_values(jax.nn.softmax(gqa_scores(qc, kc), axis=-1), vc).reshape(b, lc, B_HEADS * HEAD_DIM)
    return o_c, o_x


def diff_attention(pc, px, lq1, lk1, lq2, lk2, subln_g, lam_init, cos, sin, need_ctx):
    b, s, _ = px["c_q"].shape
    lc = pc["c_k"].shape[1]

    def qk_lat(t):
        return rope(t.reshape(b, s, 2 * C_HEADS, HEAD_DIM), cos, sin).reshape(b, s, C_HEADS, 2, HEAD_DIM)

    q = qk_lat(px["c_q"])
    k = qk_lat(px["c_k"])
    v = px["c_v"].reshape(b, s, C_HEADS, 2 * HEAD_DIM)
    kc = pc["c_k"].reshape(b, lc, C_HEADS, 2, HEAD_DIM)
    vc = pc["c_v"].reshape(b, lc, C_HEADS, 2 * HEAD_DIM)
    k_all = jnp.concatenate([kc, k], axis=1)
    v_all = jnp.concatenate([vc, v], axis=1)
    f32 = jnp.float32
    lam = (jnp.exp(jnp.sum(lq1.astype(f32) * lk1.astype(f32)))
           - jnp.exp(jnp.sum(lq2.astype(f32) * lk2.astype(f32))) + lam_init)

    def attend(qb, keys, vals):
        sc = jnp.einsum("bqhtd,bshtd->bhtqs", qb, keys).astype(f32) * (HEAD_DIM ** -0.5)
        p = jax.nn.softmax(sc, axis=-1)
        w = (p[:, :, 0] - lam * p[:, :, 1]).astype(vals.dtype)
        o = jnp.einsum("bhqs,bshe->bqhe", w, vals)
        o = rmsnorm(o, subln_g) * (1.0 - lam_init)
        return o.reshape(o.shape[0], o.shape[1], C_HEADS * 2 * HEAD_DIM)

    def block(n):
        return attend(lax.dynamic_slice_in_dim(q, n * BLK, BLK, axis=1), k_all, v_all)

    o_x = sweep_query_blocks(block, s)
    o_c = None
    if need_ctx:
        o_c = attend(pc["c_q"].reshape(b, lc, C_HEADS, 2, HEAD_DIM), kc, vc)
    return o_c, o_x


def dwconv_centred(u, w, bias):
    ch = u.shape[-1]
    pad = CONV_W // 2
    y = lax.conv_general_dilated(u, w[:, None, :].astype(u.dtype), (1,), ((pad, pad),),
                                 dimension_numbers=("NWC", "WIO", "NWC"), feature_group_count=ch)
    return y + bias.astype(u.dtype)


def segsum_exp(a):
    cs = jnp.cumsum(a, axis=-1)
    t = a.shape[-1]
    mask = jnp.tril(jnp.ones((t, t), dtype=bool))
    return jnp.exp(jnp.where(mask, cs[..., :, None] - cs[..., None, :], -jnp.inf))


def ssd_scan(xs, dt, A, Bm, Cm, h0, need_y):
    b, L, g, e, p = xs.shape
    n = Bm.shape[-1]
    nc = L // M_CHUNK
    X = (xs * dt[..., None]).reshape(b, nc, M_CHUNK, g, e, p)
    a = jnp.moveaxis((dt * A).reshape(b, nc, M_CHUNK, g, e), 2, -1)
    a_cs = jnp.cumsum(a, axis=-1)
    Bc = Bm.reshape(b, nc, M_CHUNK, g, n)
    Cc = Cm.reshape(b, nc, M_CHUNK, g, n)
    decay_to_end = jnp.exp(a_cs[..., -1:] - a_cs)
    states = jnp.einsum("bclgn,bcgel,bclgep->bcgepn", Bc, decay_to_end, X)
    chunk_a = jnp.pad(a_cs[..., -1], ((0, 0), (1, 0), (0, 0), (0, 0)))
    decay_chunk = segsum_exp(jnp.moveaxis(chunk_a, 1, -1))
    all_states = jnp.concatenate([h0[:, None], states], axis=1)
    new_states = jnp.einsum("bgezc,bcgepn->bzgepn", decay_chunk, all_states)
    final = new_states[:, -1]
    if not need_y:
        return None, final
    Lmat = segsum_exp(a)
    CB = jnp.einsum("bclgn,bcsgn->bcgls", Cc, Bc)
    y_diag = jnp.einsum("bcgels,bcsgep->bclgep", CB[:, :, :, None] * Lmat, X)
    y_off = jnp.einsum("bclgn,bcgepn,bcgel->bclgep", Cc, new_states[:, :-1], jnp.exp(a_cs))
    return (y_diag + y_off).reshape(b, L, g, e, p), final


def mamba2_bidir(pc, px, conv_w, conv_b, dt_bias, a_log, d_skip, norm_g, need_ctx):
    f32 = jnp.float32
    gn = M_GROUPS * M_STATE
    A = -jnp.exp(a_log.astype(f32)).reshape(2, M_GROUPS, M_HPG)
    dtb = dt_bias.astype(f32).reshape(2, M_GROUPS, M_HPG)
    dsk = d_skip.astype(f32).reshape(M_GROUPS, M_HPG, 1)

    def prep(p):
        bb, ll, _ = p["d_xbc"].shape
        xbc = jax.nn.silu(dwconv_centred(p["d_xbc"], conv_w, conv_b)).astype(f32)
        xs = xbc[..., :M_INNER].reshape(bb, ll, M_GROUPS, M_HPG, M_HEAD_DIM)
        Bm = xbc[..., M_INNER:M_INNER + gn].reshape(bb, ll, M_GROUPS, M_STATE)
        Cm = xbc[..., M_INNER + gn:].reshape(bb, ll, M_GROUPS, M_STATE)
        dt = jax.nn.softplus(p["d_dt"].astype(f32).reshape(bb, ll, 2, M_GROUPS, M_HPG) + dtb)
        return xs, Bm, Cm, dt

    def flip(t):
        return jnp.flip(t, axis=1)

    xc, Bc, Cc, dtc = prep(pc)
    xx, Bx, Cx, dtx = prep(px)
    h0 = jnp.zeros((xc.shape[0], M_GROUPS, M_HPG, M_HEAD_DIM, M_STATE), f32)
    yc_f, sc_f = ssd_scan(xc, dtc[:, :, 0], A[0], Bc, Cc, h0, need_ctx)
    yc_b, sc_b = ssd_scan(flip(xc), flip(dtc[:, :, 1]), A[1], flip(Bc), flip(Cc), h0, need_ctx)
    yx_f, _ = ssd_scan(xx, dtx[:, :, 0], A[0], Bx, Cx, sc_f, True)
    yx_b, _ = ssd_scan(flip(xx), flip(dtx[:, :, 1]), A[1], flip(Bx), flip(Cx), sc_b, True)

    def finish(y_f, y_b_rev, xs, z):
        bb, ll = xs.shape[:2]
        y = y_f + flip(y_b_rev) + xs * dsk
        y = y.reshape(bb, ll, M_INNER) * jax.nn.silu(z.astype(f32))
        y = rmsnorm(y.reshape(bb, ll, M_GROUPS, M_INNER // M_GROUPS)).reshape(bb, ll, M_INNER)
        return (y * norm_g.astype(f32)).astype(z.dtype)

    o_x = finish(yx_f, yx_b, xx, px["d_z"])
    o_c = finish(yc_f, yc_b, xc, pc["d_z"]) if need_ctx else None
    return o_c, o_x


def merge_branches(outs, p, w_br, w_out):
    acc = None
    off = 0
    for o, width, gname in zip(outs, BR_WIDTHS, GATE_NAMES):
        term = jax.nn.sigmoid(p[gname]) * (o @ w_br[off:off + width])
        acc = term if acc is None else acc + term
        off += width
    return acc @ w_out


def peer_ffn(h, wq, sub_keys, eu, ev):
    shape = h.shape
    tok = h.reshape(-1, P_CHUNK, shape[-1])

    def chunk(t):
        q = (t @ wq).reshape(P_CHUNK, P_HEADS, 2, P_DK // 2)
        s = jnp.einsum("thjd,hjkd->thjk", q, sub_keys).astype(jnp.float32)
        sv, si = lax.top_k(s, P_TOPK)
        cand = sv[:, :, 0, :, None] + sv[:, :, 1, None, :]
        cidx = si[:, :, 0, :, None] * N_KEYS + si[:, :, 1, None, :]
        best, pos = lax.top_k(cand.reshape(P_CHUNK, P_HEADS, P_TOPK * P_TOPK), P_TOPK)
        eidx = jnp.take_along_axis(cidx.reshape(P_CHUNK, P_HEADS, P_TOPK * P_TOPK), pos, axis=-1)
        gate = jax.nn.softmax(best, axis=-1)
        u = jnp.take(eu, eidx, axis=0)
        act = jax.nn.gelu(jnp.einsum("thkd,td->thk", u, t).astype(jnp.float32), approximate=False)
        v = jnp.take(ev, eidx, axis=0)
        return jnp.einsum("thk,thkd->td", (gate * act).astype(v.dtype), v)

    return lax.map(chunk, tok).reshape(shape)


def setup_inputs(seed: int = 0) -> dict:
    key = jax.random.key(seed)
    ks = iter(jax.random.split(key, 40))
    f32 = jnp.float32
    L, D = DEPTH, D_MODEL

    def nrm(shape, scale):
        return jax.random.normal(next(ks), shape, f32) * scale

    dt = jnp.exp(jax.random.uniform(next(ks), (L, 2, M_HEADS), f32) * (math.log(0.1) - math.log(0.001)) + math.log(0.001))
    return {
        "x": nrm((BATCH, SEQ, D), 1.0),
        "c": nrm((BATCH, D), 1.0),
        "ctx": nrm((BATCH, CTX_LEN, D), 1.0),
        "c_ctx": nrm((D,), 1.0),
        "w_ada": nrm((L, D, 6 * D), 0.5 * D ** -0.5),
        "b_ada": nrm((L, 6 * D), 0.02),
        "g_norm1": 1.0 + nrm((L, D), 0.02),
        "g_norm2": 1.0 + nrm((L, D), 0.02),
        "w_in": nrm((L, D, IN_W), D ** -0.5),
        "a_sink": nrm((L, A_HEADS), 0.5),
        "b_qnorm": 1.0 + nrm((L, HEAD_DIM), 0.02),
        "b_knorm": 1.0 + nrm((L, HEAD_DIM), 0.02),
        "c_lam_q1": nrm((L, HEAD_DIM), 0.1),
        "c_lam_k1": nrm((L, HEAD_DIM), 0.1),
        "c_lam_q2": nrm((L, HEAD_DIM), 0.1),
        "c_lam_k2": nrm((L, HEAD_DIM), 0.1),
        "c_subln": 1.0 + nrm((L, 2 * HEAD_DIM), 0.02),
        "m_conv_w": nrm((L, CONV_W, M_XBC), CONV_W ** -0.5),
        "m_conv_b": nrm((L, M_XBC), 0.02),
        "m_dt_bias": dt + jnp.log(-jnp.expm1(-dt)),
        "m_a_log": jnp.log(jax.random.uniform(next(ks), (L, 2, M_HEADS), f32, minval=1.0, maxval=16.0)),
        "m_d": 1.0 + nrm((L, M_HEADS), 0.1),
        "m_norm": 1.0 + nrm((L, M_INNER), 0.02),
        "w_br": nrm((L, MIX_W, D), (MIX_W // N_BRANCH) ** -0.5),
        "w_out": nrm((L, D, D), D ** -0.5),
        "p_wq": nrm((L, D, P_HEADS * P_DK), D ** -0.5),
        "p_subkeys": nrm((L, P_HEADS, 2, N_KEYS, P_DK // 2), (P_DK // 2) ** -0.5),
        "p_u": nrm((L, N_EXP, D), D ** -0.5),
        "p_v": nrm((L, N_EXP, D), P_HEADS ** -0.5),
        "g_final": 1.0 + nrm((D,), 0.02),
    }


def reference(x, c, ctx, c_ctx, w_ada, b_ada, g_norm1, g_norm2, w_in, a_sink, b_qnorm, b_knorm,
              c_lam_q1, c_lam_k1, c_lam_q2, c_lam_k2, c_subln, m_conv_w, m_conv_b, m_dt_bias,
              m_a_log, m_d, m_norm, w_br, w_out, p_wq, p_subkeys, p_u, p_v, g_final):
    ROWS = x.shape[1] // GRID_W
    cos, sin = axial_rope_tables(ROWS)
    for l in range(DEPTH):
        need_ctx = l < DEPTH - 1
        mx = [t[:, None, :] for t in jnp.split(jax.nn.silu(c) @ w_ada[l] + b_ada[l], 6, axis=-1)]
        mc = jnp.split(jax.nn.silu(c_ctx) @ w_ada[l] + b_ada[l], 6, axis=-1)
        h_x = modulate(rmsnorm(x, g_norm1[l]), mx[0], mx[1])
        h_c = modulate(rmsnorm(ctx, g_norm1[l]), mc[0], mc[1])
        px = in_proj(h_x, w_in[l])
        pc = in_proj(h_c, w_in[l], None if need_ctx else CTX_KV_PARTS)
        a_c, a_x = window_attention(pc, px, a_sink[l], cos, sin, need_ctx)
        b_c, b_x = qknorm_attention(pc, px, b_qnorm[l], b_knorm[l], cos, sin, need_ctx)
        lam_init = 0.8 - 0.6 * math.exp(-0.3 * l)
        d_c, d_x = diff_attention(pc, px, c_lam_q1[l], c_lam_k1[l], c_lam_q2[l], c_lam_k2[l],
                                  c_subln[l], lam_init, cos, sin, need_ctx)
        s_c, s_x = mamba2_bidir(pc, px, m_conv_w[l], m_conv_b[l], m_dt_bias[l], m_a_log[l],
                                m_d[l], m_norm[l], need_ctx)
        x = x + mx[2] * merge_branches([a_x, b_x, d_x, s_x], px, w_br[l], w_out[l])
        if need_ctx:
            ctx = ctx + mc[2] * merge_branches([a_c, b_c, d_c, s_c], pc, w_br[l], w_out[l])
        x = x + mx[5] * peer_ffn(modulate(rmsnorm(x, g_norm2[l]), mx[3], mx[4]),
                                 p_wq[l], p_subkeys[l], p_u[l], p_v[l])
        if need_ctx:
            ctx = ctx + mc[5] * peer_ffn(modulate(rmsnorm(ctx, g_norm2[l]), mc[3], mc[4]),
                                         p_wq[l], p_subkeys[l], p_u[l], p_v[l])
    return rmsnorm(x, g_final)
```

```python
import functools
import math

import jax
import jax.numpy as jnp
from jax import lax
from jax.experimental import pallas as pl
from jax.experimental.pallas import tpu as pltpu

F32 = jnp.float32
CDT = jnp.bfloat16

EPS = 1e-6
NEG = -1e30
HEAD_DIM = 64
GRID_W = 64
ROPE_THETA = 10000.0
WINDOW = 128
A_HEADS, A_KV = 8, 2
B_HEADS, B_KV = 8, 2
C_HEADS = 4
M_HEADS, M_HEAD_DIM, M_GROUPS, M_STATE = 16, 64, 2, 128
M_INNER = M_HEADS * M_HEAD_DIM
M_HPG = M_HEADS // M_GROUPS
CHUNK = 128
P_HEADS, N_KEYS, P_TOPK = 8, 128, 16
TOK_TILE = 256

VMEM_LIMIT = 48 * 1024 * 1024


def _params(sem):
    return pltpu.CompilerParams(dimension_semantics=sem, vmem_limit_bytes=VMEM_LIMIT)


def _mm_kernel(a_ref, b_ref, o_ref):
    o_ref[...] = jnp.dot(a_ref[...], b_ref[...], preferred_element_type=F32).astype(o_ref.dtype)


def _mm_bias_kernel(a_ref, b_ref, bias_ref, o_ref):
    acc = jnp.dot(a_ref[...], b_ref[...], preferred_element_type=F32)
    o_ref[...] = (acc + bias_ref[...]).astype(o_ref.dtype)


def _matmul(a, b, out_dtype, tm, tn, bias=None):
    m, k = a.shape
    n = b.shape[1]
    assert m % tm == 0 and n % tn == 0, (a.shape, b.shape, tm, tn)
    in_specs = [pl.BlockSpec((tm, k), lambda i, j: (i, 0)),
                pl.BlockSpec((k, tn), lambda i, j: (0, j))]
    args = [a, b]
    kern = _mm_kernel
    if bias is not None:
        in_specs.append(pl.BlockSpec((1, tn), lambda i, j: (0, j)))
        args.append(bias)
        kern = _mm_bias_kernel
    return pl.pallas_call(
        kern, grid=(m // tm, n // tn), in_specs=in_specs,
        out_specs=pl.BlockSpec((tm, tn), lambda i, j: (i, j)),
        out_shape=jax.ShapeDtypeStruct((m, n), out_dtype),
        compiler_params=_params(("parallel", "parallel")), name="matmul")(*args)


def _norm_mod_kernel(*refs, has_delta, want_t):
    it = iter(refs)
    x_ref = next(it)
    if has_delta:
        d_ref, gp_ref = next(it), next(it)
    g_ref, sh_ref, sc_ref = next(it), next(it), next(it)
    if has_delta:
        xo_ref = next(it)
    h_ref = next(it)
    x = x_ref[...]
    if has_delta:
        x = x + gp_ref[0] * d_ref[...]
        xo_ref[...] = x
    y = x * lax.rsqrt(jnp.mean(x * x, axis=-1, keepdims=True) + EPS) * g_ref[...]
    h = y * (1.0 + sc_ref[0]) + sh_ref[0]
    h_ref[...] = h.astype(h_ref.dtype)
    if want_t:
        ht_ref = next(it)
        ht_ref[...] = h.T.astype(ht_ref.dtype)


def _norm_mod(x, delta, gate_prev, g, shift, scale, tiles_per_batch, want_t=False, out_dtype=None):
    n, d = x.shape
    tt = TOK_TILE
    has_delta = delta is not None
    out_dtype = out_dtype or CDT

    def mod_idx(i):
        return ((i // tiles_per_batch) * 2 + jnp.minimum(i % tiles_per_batch, 1), 0, 0)

    row = pl.BlockSpec((tt, d), lambda i: (i, 0))
    mod = pl.BlockSpec((1, 1, d), mod_idx)
    in_specs, args = [row], [x]
    if has_delta:
        in_specs += [row, mod]
        args += [delta, gate_prev]
    in_specs += [pl.BlockSpec((1, d), lambda i: (0, 0)), mod, mod]
    args += [g.reshape(1, d), shift, scale]
    out_specs, out_shape = [], []
    if has_delta:
        out_specs.append(row)
        out_shape.append(jax.ShapeDtypeStruct((n, d), F32))
    out_specs.append(row)
    out_shape.append(jax.ShapeDtypeStruct((n, d), out_dtype))
    if want_t:
        out_specs.append(pl.BlockSpec((d, tt), lambda i: (0, i)))
        out_shape.append(jax.ShapeDtypeStruct((d, n), CDT))
    outs = pl.pallas_call(
        functools.partial(_norm_mod_kernel, has_delta=has_delta, want_t=want_t),
        grid=(n // tt,), in_specs=in_specs, out_specs=out_specs, out_shape=out_shape,
        compiler_params=_params(("parallel",)), name="norm_mod")(*args)
    outs = list(outs)
    x_new = outs.pop(0) if has_delta else x
    return (x_new, *outs)


_L2_SIZES = tuple(16 if a == 0 else 8 for a in range(P_TOPK))
_L2_VALID = tuple(P_TOPK // (a + 1) for a in range(P_TOPK))
_L2_ROWS = sum(_L2_SIZES)


def _top16(s, row):
    n, t = s.shape
    rank = jnp.full((n, t), 127.0, F32)
    vals = []
    work = s
    for a in range(P_TOPK):
        m = jnp.max(work, axis=0, keepdims=True)
        idx = jnp.min(jnp.where(work == m, row, n), axis=0, keepdims=True)
        hit = row == idx
        rank = jnp.where(hit, float(a), rank)
        work = jnp.where(hit, -jnp.inf, work)
        vals.append(m)
    return jnp.concatenate(vals, axis=0), rank


def _route_kernel(qt_ref, sk_ref, e1_ref, c1_ref, e2_ref, r2_ref):
    tt = qt_ref.shape[1]
    row = lax.broadcasted_iota(jnp.int32, (N_KEYS, tt), 0)
    s1 = jnp.dot(sk_ref[0, 0], qt_ref[0:N_KEYS, :], preferred_element_type=F32)
    s2 = jnp.dot(sk_ref[0, 1], qt_ref[N_KEYS:2 * N_KEYS, :], preferred_element_type=F32)
    sv1, rank1 = _top16(s1, row)
    sv2, rank2 = _top16(s2, row)
    pieces = []
    for a in range(P_TOPK):
        sz = _L2_SIZES[a]
        b_iota = lax.broadcasted_iota(jnp.int32, (sz, tt), 0)
        pieces.append(jnp.where(b_iota < _L2_VALID[a], sv1[a:a + 1, :] + sv2[0:sz, :], -jnp.inf))
    cand = jnp.concatenate(pieces, axis=0)
    crow = lax.broadcasted_iota(jnp.int32, (_L2_ROWS, tt), 0)
    best0 = sv1[0:1, :] + sv2[0:1, :]
    sel = jnp.zeros((_L2_ROWS, tt), F32)
    z = jnp.zeros((1, tt), F32)
    for _ in range(P_TOPK):
        m = jnp.max(cand, axis=0, keepdims=True)
        pos = jnp.min(jnp.where(cand == m, crow, _L2_ROWS), axis=0, keepdims=True)
        hit = crow == pos
        sel = jnp.where(hit, 1.0, sel)
        cand = jnp.where(hit, -jnp.inf, cand)
        z = z + jnp.exp(m - best0)
    cnt1 = jnp.zeros((N_KEYS, tt), F32)
    off = 0
    for a in range(P_TOPK):
        c_a = jnp.sum(sel[off:off + _L2_SIZES[a], :], axis=0, keepdims=True)
        cnt1 = jnp.where(rank1 == float(a), c_a, cnt1)
        off += _L2_SIZES[a]
    e1_ref[0] = jnp.exp(s1 - sv1[0:1, :])
    c1_ref[0] = cnt1
    e2_ref[0] = jnp.exp(s2 - sv2[0:1, :]) / z
    r2_ref[0] = rank2


def _peer_route(qt, sub_keys, tt):
    n = qt.shape[1]
    dk = sub_keys.shape[-1]
    out = jax.ShapeDtypeStruct((P_HEADS, N_KEYS, n), F32)
    ospec = pl.BlockSpec((1, N_KEYS, tt), lambda i, h: (h, 0, i))
    return pl.pallas_call(
        _route_kernel, grid=(n // tt, P_HEADS),
        in_specs=[pl.BlockSpec((2 * dk, tt), lambda i, h: (h, i)),
                  pl.BlockSpec((1, 2, N_KEYS, dk), lambda i, h: (h, 0, 0, 0))],
        out_specs=[ospec] * 4, out_shape=[out] * 4,
        compiler_params=_params(("parallel", "parallel")), name="peer_route")(qt, sub_keys)


def _gelu(x):
    return 0.5 * x * (1.0 + lax.erf(x * (2.0 ** -0.5)))


def _peer_dense_kernel(ht_ref, eu_ref, evt_ref, e1_ref, c1_ref, e2_ref, r2_ref, o_ref, acc_ref, *, rows):
    k = pl.program_id(1)

    @pl.when(k == 0)
    def _():
        acc_ref[...] = jnp.zeros_like(acc_ref)

    act = _gelu(jnp.dot(eu_ref[...], ht_ref[...], preferred_element_type=F32))
    pieces = []
    for ii in range(rows):
        i = k * rows + ii
        g = None
        for h in range(P_HEADS):
            cnt = c1_ref[h, pl.ds(i, 1), :]
            e1 = e1_ref[h, pl.ds(i, 1), :]
            term = jnp.where(r2_ref[h] < cnt, e2_ref[h], 0.0) * e1
            g = term if g is None else g + term
        pieces.append(g)
    gate = jnp.concatenate(pieces, axis=0)
    w = (gate * act).astype(evt_ref.dtype)
    acc_ref[...] += jnp.dot(evt_ref[...], w, preferred_element_type=F32)

    @pl.when(k == pl.num_programs(1) - 1)
    def _():
        o_ref[...] = acc_ref[...].T


def _peer_dense(ht, eu, evt, e1, c1, e2, r2, tt, rows):
    d, n = ht.shape
    n_exp = eu.shape[0]
    eb = rows * N_KEYS
    rt = pl.BlockSpec((P_HEADS, N_KEYS, tt), lambda i, k: (0, 0, i))
    return pl.pallas_call(
        functools.partial(_peer_dense_kernel, rows=rows),
        grid=(n // tt, n_exp // eb),
        in_specs=[pl.BlockSpec((d, tt), lambda i, k: (0, i)),
                  pl.BlockSpec((eb, d), lambda i, k: (k, 0)),
                  pl.BlockSpec((d, eb), lambda i, k: (0, k)),
                  rt, rt, rt, rt],
        out_specs=pl.BlockSpec((tt, d), lambda i, k: (i, 0)),
        out_shape=jax.ShapeDtypeStruct((n, d), F32),
        scratch_shapes=[pltpu.VMEM((d, tt), F32)],
        compiler_params=_params(("parallel", "arbitrary")), name="peer_dense")(ht, eu, evt, e1, c1, e2, r2)


def _peer(ht, wqt, sub_keys, eu, evt):
    n = ht.shape[1]
    tt = 256 if n % 256 == 0 else 128
    qt = _matmul(wqt, ht, CDT, 512, tt)
    e1, c1, e2, r2 = _peer_route(qt, sub_keys, tt)
    return _peer_dense(ht, eu, evt, e1, c1, e2, r2, tt, rows=4)


_SRC = {}
_off = 0
for _name, _w in (("a_q", 512), ("a_k", 128), ("a_v", 128), ("b_q", 512), ("b_k", 128), ("b_v", 128),
                  ("c_q", 512), ("c_k", 512), ("c_v", 512), ("d_z", 1024), ("d_xs", 1024), ("d_b", 256),
                  ("d_c", 256), ("d_dt", 32), ("g_a", 1024), ("g_b", 1024), ("g_c", 1024), ("g_d", 1024)):
    _SRC[_name] = (_off, _w)
    _off += _w
_ORDER = ("d_z", "d_xs", "g_a", "g_b", "g_c", "g_d", "a_q", "b_q", "c_q", "c_k", "c_v",
          "d_b", "d_c", "a_k", "a_v", "b_k", "b_v")
_DST = {}
_off = 0
for _name in _ORDER:
    _DST[_name] = _off
    _off += _SRC[_name][1]
P_WIDTH = _off
DT_PAD = 128


def _blk(name, width):
    return _DST[name] // width


def _rope(x, cos, sin):
    w = x.shape[1]
    lane = lax.broadcasted_iota(jnp.int32, x.shape, 1)
    fwd = pltpu.roll(x, 16, 1)
    bwd = pltpu.roll(x, w - 16, 1)
    rot = jnp.where(lane % 32 < 16, -bwd, fwd)
    return x * cos + rot * sin


def _head_rms(x, bd):
    sq = x * x
    hi = sq.astype(jnp.bfloat16)
    lo = (sq - hi.astype(F32)).astype(jnp.bfloat16)
    ms = jnp.dot(hi, bd, preferred_element_type=F32) + jnp.dot(lo, bd, preferred_element_type=F32)
    return lax.rsqrt(ms + EPS)


def _qk_prep_kernel(q_ref, kv_ref, cs_ref, sn_ref, qg_ref, kg_ref, bd_ref, qo_ref, kvo_ref):
    cos2, sin2 = cs_ref[...], sn_ref[...]
    cos4 = jnp.concatenate([cos2] * 4, axis=1)
    sin4 = jnp.concatenate([sin2] * 4, axis=1)
    scale = HEAD_DIM ** -0.5
    bd = bd_ref[...]
    a_q = q_ref[:, 0:512].astype(F32)
    qo_ref[:, 0:512] = (_rope(a_q, cos4, sin4) * scale).astype(qo_ref.dtype)
    b_q = q_ref[:, 512:1024].astype(F32)
    b_q = b_q * _head_rms(b_q, bd) * qg_ref[...]
    qo_ref[:, 512:1024] = (_rope(b_q, cos4, sin4) * scale).astype(qo_ref.dtype)
    c_q = q_ref[:, 1024:1536].astype(F32)
    qo_ref[:, 1024:1536] = (_rope(c_q, cos4, sin4) * scale).astype(qo_ref.dtype)
    c_k = q_ref[:, 1536:2048].astype(F32)
    qo_ref[:, 1536:2048] = _rope(c_k, cos4, sin4).astype(qo_ref.dtype)
    a_k = kv_ref[:, 0:128].astype(F32)
    kvo_ref[:, 0:128] = _rope(a_k, cos2, sin2).astype(kvo_ref.dtype)
    kvo_ref[:, 128:256] = kv_ref[:, 128:256]
    b_k = kv_ref[:, 256:384].astype(F32)
    b_k = b_k * _head_rms(b_k, bd[0:128, 0:128]) * kg_ref[...]
    kvo_ref[:, 256:384] = _rope(b_k, cos2, sin2).astype(kvo_ref.dtype)
    kvo_ref[:, 384:512] = kv_ref[:, 384:512]


def _qk_prep(p, cos, sin, qn_g, kn_g, tiles_per_batch):
    n = p.shape[0]
    tt = TOK_TILE
    lane = jnp.arange(512)
    bd = jnp.where(lane[:, None] // HEAD_DIM == lane[None, :] // HEAD_DIM, 1.0 / HEAD_DIM, 0.0).astype(jnp.bfloat16)
    pos = pl.BlockSpec((tt, 128), lambda i: (i % tiles_per_batch, 0))
    full = lambda r, c: pl.BlockSpec((r, c), lambda i: (0, 0))
    return pl.pallas_call(
        _qk_prep_kernel, grid=(n // tt,),
        in_specs=[pl.BlockSpec((tt, 2048), lambda i: (i, _blk("a_q", 2048))),
                  pl.BlockSpec((tt, 512), lambda i: (i, _blk("a_k", 512))),
                  pos, pos, full(1, 512), full(1, 128), full(512, 512)],
        out_specs=[pl.BlockSpec((tt, 2048), lambda i: (i, 0)), pl.BlockSpec((tt, 512), lambda i: (i, 0))],
        out_shape=[jax.ShapeDtypeStruct((n, 2048), CDT), jax.ShapeDtypeStruct((n, 512), CDT)],
        compiler_params=_params(("parallel",)), name="qk_prep")(
            p, p, cos, sin, jnp.tile(qn_g, 8).reshape(1, 512), jnp.tile(kn_g, 2).reshape(1, 128), bd)


_NT = (((1,), (1,)), ((), ()))


def _stack_heads(ref, first, count):
    return jnp.concatenate([ref[:, (first + i) * HEAD_DIM:(first + i + 1) * HEAD_DIM] for i in range(count)], axis=0)


def _attn_a_kernel(q_ref, k_ref, v_ref, sink_ref, o_ref, *, n_ctx):
    tq = q_ref.shape[0]
    total = k_ref.shape[0]
    t = pl.program_id(1)
    n = t - n_ctx // tq
    kstart = pl.multiple_of(jnp.clip(n_ctx + (n - 1) * tq, n_ctx - tq, total - 3 * tq), tq)
    qpos = n * tq + lax.broadcasted_iota(jnp.int32, (tq, 3 * tq), 0)
    kpos = kstart - n_ctx + lax.broadcasted_iota(jnp.int32, (tq, 3 * tq), 1)
    valid = (jnp.abs(qpos - kpos) <= WINDOW) & (kpos >= 0) & (n >= 0)
    groups = A_HEADS // A_KV
    for g in range(A_KV):
        q = _stack_heads(q_ref, g * groups, groups)
        hs = slice(g * HEAD_DIM, (g + 1) * HEAD_DIM)
        kc, vc = k_ref[0:n_ctx, hs], v_ref[0:n_ctx, hs]
        kl, vl = k_ref[pl.ds(kstart, 3 * tq), hs], v_ref[pl.ds(kstart, 3 * tq), hs]
        sc = lax.dot_general(q, kc, _NT, preferred_element_type=F32)
        sl = lax.dot_general(q, kl, _NT, preferred_element_type=F32)
        for i in range(groups):
            rows = slice(i * tq, (i + 1) * tq)
            hq = g * groups + i
            sink = sink_ref[hq:hq + 1, 0:1]
            sci = sc[rows]
            sli = jnp.where(valid, sl[rows], NEG)
            m = jnp.maximum(jnp.maximum(jnp.max(sci, axis=1, keepdims=True), jnp.max(sli, axis=1, keepdims=True)), sink)
            ec = jnp.exp(sci - m)
            el = jnp.exp(sli - m)
            den = jnp.sum(ec, axis=1, keepdims=True) + jnp.sum(el, axis=1, keepdims=True) + jnp.exp(sink - m)
            o = jnp.dot(ec.astype(vc.dtype), vc, preferred_element_type=F32)
            o = o + jnp.dot(el.astype(vl.dtype), vl, preferred_element_type=F32)
            o_ref[:, hq * HEAD_DIM:(hq + 1) * HEAD_DIM] = (o / den).astype(o_ref.dtype)


def _attn_a(q2, kv2, sink, batch, n_ctx):
    n = q2.shape[0]
    total = n // batch
    tq = WINDOW
    tpb = total // tq
    kvspec = lambda c: pl.BlockSpec((total, 128), lambda b, t: (b, c))
    return pl.pallas_call(
        functools.partial(_attn_a_kernel, n_ctx=n_ctx), grid=(batch, tpb),
        in_specs=[pl.BlockSpec((tq, 512), lambda b, t: (b * tpb + t, 0)), kvspec(0), kvspec(1),
                  pl.BlockSpec((A_HEADS, 128), lambda b, t: (0, 0))],
        out_specs=pl.BlockSpec((tq, 512), lambda b, t: (b * tpb + t, 0)),
        out_shape=jax.ShapeDtypeStruct((n, 512), CDT),
        compiler_params=_params(("parallel", "parallel")), name="attn_a")(
            q2, kv2, kv2, jnp.broadcast_to(sink.astype(F32)[:, None], (A_HEADS, 128)))


def _flash_step(q, k, v, m, l, acc):
    s = lax.dot_general(q, k, _NT, preferred_element_type=F32)
    m_new = jnp.maximum(m, jnp.max(s, axis=1, keepdims=True))
    alpha = jnp.exp(m - m_new)
    p = jnp.exp(s - m_new)
    l = alpha * l + jnp.sum(p, axis=1, keepdims=True)
    acc = alpha * acc + jnp.dot(p.astype(v.dtype), v, preferred_element_type=F32)
    return m_new, l, acc


def _flash_init(rows, width):
    return jnp.full((rows, 1), NEG, F32), jnp.zeros((rows, 1), F32), jnp.zeros((rows, width), F32)


def _num_key_chunks(k_ref, kc, n_ctx):
    return jnp.where(pl.program_id(1) == 0, n_ctx // kc, k_ref.shape[0] // kc)


def _attn_b_kernel(q_ref, k_ref, v_ref, o_ref, *, n_ctx, kc):
    tq = q_ref.shape[0]
    nk = _num_key_chunks(k_ref, kc, n_ctx)
    groups = B_HEADS // B_KV
    for g in range(B_KV):
        q = _stack_heads(q_ref, g * groups, groups)
        hs = slice(g * HEAD_DIM, (g + 1) * HEAD_DIM)

        def body(j, carry):
            ks = pl.multiple_of(j * kc, kc)
            return _flash_step(q, k_ref[pl.ds(ks, kc), hs], v_ref[pl.ds(ks, kc), hs], *carry)

        _, l, acc = lax.fori_loop(0, nk, body, _flash_init(groups * tq, HEAD_DIM))
        o = acc / l
        for i in range(groups):
            hq = g * groups + i
            o_ref[:, hq * HEAD_DIM:(hq + 1) * HEAD_DIM] = o[i * tq:(i + 1) * tq].astype(o_ref.dtype)


def _attn_b(q2, kv2, batch, n_ctx):
    n = q2.shape[0]
    total = n // batch
    tq = TOK_TILE
    tpb = total // tq
    kvspec = lambda c: pl.BlockSpec((total, 128), lambda b, t: (b, c))
    return pl.pallas_call(
        functools.partial(_attn_b_kernel, n_ctx=n_ctx, kc=256), grid=(batch, tpb),
        in_specs=[pl.BlockSpec((tq, 512), lambda b, t: (b * tpb + t, 1)), kvspec(2), kvspec(3)],
        out_specs=pl.BlockSpec((tq, 512), lambda b, t: (b * tpb + t, 0)),
        out_shape=jax.ShapeDtypeStruct((n, 512), CDT),
        compiler_params=_params(("parallel", "parallel")), name="attn_b")(q2, kv2, kv2)


def _attn_c_kernel(q_ref, k_ref, v_ref, lam_ref, g_ref, o_ref, *, n_ctx, kc, out_scale):
    tq = q_ref.shape[0]
    nk = _num_key_chunks(k_ref, kc, n_ctx)
    lam = lam_ref[...]
    for h in range(C_HEADS):
        c0 = 2 * h * HEAD_DIM
        q0, q1 = q_ref[:, c0:c0 + HEAD_DIM], q_ref[:, c0 + HEAD_DIM:c0 + 2 * HEAD_DIM]

        def body(j, carry):
            ks = pl.multiple_of(j * kc, kc)
            v = v_ref[pl.ds(ks, kc), c0:c0 + 2 * HEAD_DIM]
            st0 = _flash_step(q0, k_ref[pl.ds(ks, kc), c0:c0 + HEAD_DIM], v, *carry[0])
            st1 = _flash_step(q1, k_ref[pl.ds(ks, kc), c0 + HEAD_DIM:c0 + 2 * HEAD_DIM], v, *carry[1])
            return st0, st1

        init = _flash_init(tq, 2 * HEAD_DIM)
        (_, l0, acc0), (_, l1, acc1) = lax.fori_loop(0, nk, body, (init, init))
        o = acc0 / l0 - lam * (acc1 / l1)
        o = o * lax.rsqrt(jnp.mean(o * o, axis=-1, keepdims=True) + EPS) * g_ref[...] * out_scale
        o_ref[:, c0:c0 + 2 * HEAD_DIM] = o.astype(o_ref.dtype)


def _attn_c(q2, p, lam, subln_g, lam_init, batch, n_ctx):
    n = q2.shape[0]
    total = n // batch
    tq = TOK_TILE
    tpb = total // tq
    vec = pl.BlockSpec((1, 128), lambda b, t: (0, 0))
    return pl.pallas_call(
        functools.partial(_attn_c_kernel, n_ctx=n_ctx, kc=256, out_scale=1.0 - lam_init), grid=(batch, tpb),
        in_specs=[pl.BlockSpec((tq, 512), lambda b, t: (b * tpb + t, 2)),
                  pl.BlockSpec((total, 512), lambda b, t: (b, 3)),
                  pl.BlockSpec((total, 512), lambda b, t: (b, _blk("c_v", 512))), vec, vec],
        out_specs=pl.BlockSpec((tq, 512), lambda b, t: (b * tpb + t, 0)),
        out_shape=jax.ShapeDtypeStruct((n, 512), CDT),
        compiler_params=_params(("parallel", "parallel")), name="attn_c")(
            q2, q2, p, jnp.broadcast_to(lam.astype(F32), (1, 128)), subln_g.astype(F32).reshape(1, 128))


def _silu(x):
    return x / (1.0 + jnp.exp(-x))


def _conv3(cur, prev_row, next_row, w_ref, b_ref):
    t = cur.shape[0]
    row = lax.broadcasted_iota(jnp.int32, cur.shape, 0)
    up = jnp.where(row == 0, prev_row, pltpu.roll(cur, 1, 0))
    dn = jnp.where(row == t - 1, next_row, pltpu.roll(cur, t - 1, 0))
    return up * w_ref[0:1, :] + cur * w_ref[1:2, :] + dn * w_ref[2:3, :] + b_ref[...]


def _ssd_prep_kernel(xs_ref, xsp_ref, xsn_ref, bc_ref, bcp_ref, bcn_ref, dt_ref, wx_ref, bx_ref, wbc_ref, bbc_ref,
                     dtb_ref, a_ref, xo_ref, bco_ref, dto_ref, ao_ref, *, tiles_per_batch):
    tb = pl.program_id(0) % tiles_per_batch
    has_prev = (tb >= 2).astype(F32)
    has_next = ((tb != 0) & (tb != tiles_per_batch - 1)).astype(F32)
    hl = xsp_ref.shape[0]
    xs = _conv3(xs_ref[...].astype(F32), xsp_ref[hl - 1:hl, :].astype(F32) * has_prev,
                xsn_ref[0:1, :].astype(F32) * has_next, wx_ref, bx_ref)
    xo_ref[...] = _silu(xs)
    bc = _conv3(bc_ref[...].astype(F32), bcp_ref[hl - 1:hl, :].astype(F32) * has_prev,
                bcn_ref[0:1, :].astype(F32) * has_next, wbc_ref, bbc_ref)
    bco_ref[...] = _silu(bc).astype(bco_ref.dtype)
    u = dt_ref[...] + dtb_ref[...]
    dt = jnp.maximum(u, 0.0) + jnp.log1p(jnp.exp(-jnp.abs(u)))
    dto_ref[...] = dt
    ao_ref[...] = dt * a_ref[...]


def _ssd_prep(p, dt_raw, conv_w, conv_b, dt_bias, a_log, tiles_per_batch):
    n = p.shape[0]
    tt = TOK_TILE
    halo = 16
    per = tt // halo
    last = n // halo - 1
    cur = lambda name, w: pl.BlockSpec((tt, w), lambda i: (i, _blk(name, w)))
    prv = lambda name, w: pl.BlockSpec((halo, w), lambda i: (jnp.maximum(i * per - 1, 0), _blk(name, w)))
    nxt = lambda name, w: pl.BlockSpec((halo, w), lambda i: (jnp.minimum((i + 1) * per, last), _blk(name, w)))
    full = lambda r, c: pl.BlockSpec((r, c), lambda i: (0, 0))
    pad = lambda v: jnp.pad(v.astype(F32).reshape(1, -1), ((0, 0), (0, DT_PAD - 2 * M_HEADS)))
    cw, cb = conv_w.astype(F32), conv_b.astype(F32).reshape(1, -1)
    row = lambda w: pl.BlockSpec((tt, w), lambda i: (i, 0))
    return pl.pallas_call(
        functools.partial(_ssd_prep_kernel, tiles_per_batch=tiles_per_batch), grid=(n // tt,),
        in_specs=[cur("d_xs", 1024), prv("d_xs", 1024), nxt("d_xs", 1024),
                  cur("d_b", 512), prv("d_b", 512), nxt("d_b", 512), row(DT_PAD),
                  full(3, 1024), full(1, 1024), full(3, 512), full(1, 512), full(1, DT_PAD), full(1, DT_PAD)],
        out_specs=[row(1024), row(512), row(DT_PAD), row(DT_PAD)],
        out_shape=[jax.ShapeDtypeStruct((n, 1024), F32), jax.ShapeDtypeStruct((n, 512), CDT),
                   jax.ShapeDtypeStruct((n, DT_PAD), F32), jax.ShapeDtypeStruct((n, DT_PAD), F32)],
        compiler_params=_params(("parallel",)), name="ssd_prep")(
            p, p, p, p, p, p, dt_raw, cw[:, :M_INNER], cb[:, :M_INNER], cw[:, M_INNER:], cb[:, M_INNER:],
            pad(dt_bias), pad(-jnp.exp(a_log.astype(F32))))


def _split3(x):
    hi = x.astype(jnp.bfloat16)
    r = x - hi.astype(F32)
    mid = r.astype(jnp.bfloat16)
    lo = (r - mid.astype(F32)).astype(jnp.bfloat16)
    return hi, mid, lo


def _expand_heads(v, cols, rows):
    return jnp.concatenate([jnp.broadcast_to(v[:, c:c + 1], (rows, M_HEAD_DIM)) for c in cols], axis=1)


def _ssd_kernel(*refs, rev):
    if rev:
        xs_ref, bc_ref, dt_ref, a_ref, yf_ref, z_ref, dsk_ref, ng_ref, o_ref, st_ref = refs
    else:
        xs_ref, bc_ref, dt_ref, a_ref, o_ref, st_ref = refs
    t = xs_ref.shape[0]

    @pl.when(pl.program_id(1) == 0)
    def _():
        st_ref[...] = jnp.zeros_like(st_ref)

    li = lax.broadcasted_iota(jnp.int32, (t, t), 0)
    si = lax.broadcasted_iota(jnp.int32, (t, t), 1)
    tri = (si >= li) if rev else (li >= si)
    tri_b = jnp.where(tri, 1.0, 0.0).astype(jnp.bfloat16)
    cs = sum(jnp.dot(tri_b, part, preferred_element_type=F32) for part in _split3(a_ref[...]))
    cst = cs.T
    tot = cs[0:1, :] if rev else cs[t - 1:t, :]
    outdec, indec, sdec = jnp.exp(cs), jnp.exp(tot - cs), jnp.exp(tot)
    dt = dt_ref[...]
    d0 = M_HEADS if rev else 0
    ys = []
    for g in range(M_GROUPS):
        cols = [d0 + g * M_HPG + e for e in range(M_HPG)]
        lanes = slice(g * M_HPG * M_HEAD_DIM, (g + 1) * M_HPG * M_HEAD_DIM)
        bg = bc_ref[:, g * M_STATE:(g + 1) * M_STATE]
        cg = bc_ref[:, (M_GROUPS + g) * M_STATE:(M_GROUPS + g + 1) * M_STATE]
        cb = lax.dot_general(cg, bg, _NT, preferred_element_type=F32)
        bgt = bg.astype(F32).T.astype(bg.dtype)
        x = xs_ref[:, lanes] * _expand_heads(dt, cols, t)
        xb = x.astype(bg.dtype)
        hst = st_ref[g]
        y_off = jnp.dot(cg, hst.astype(cg.dtype), preferred_element_type=F32) * _expand_heads(outdec, cols, t)
        y_diag = []
        for e in range(M_HPG):
            c = cols[e]
            lm = jnp.exp(jnp.where(tri, cs[:, c:c + 1] - cst[c:c + 1, :], NEG))
            y_diag.append(jnp.dot((cb * lm).astype(xb.dtype), xb[:, e * M_HEAD_DIM:(e + 1) * M_HEAD_DIM],
                                  preferred_element_type=F32))
        ys.append(y_off + jnp.concatenate(y_diag, axis=1))
        xin = (x * _expand_heads(indec, cols, t)).astype(bg.dtype)
        st_ref[g] = hst * _expand_heads(sdec, cols, 1) + jnp.dot(bgt, xin, preferred_element_type=F32)
    y = jnp.concatenate(ys, axis=1)
    if not rev:
        o_ref[...] = y
        return
    y = (y + yf_ref[...] + xs_ref[...] * dsk_ref[...]) * _silu(z_ref[...].astype(F32))
    half = M_INNER // M_GROUPS
    parts = []
    for g in range(M_GROUPS):
        yg = y[:, g * half:(g + 1) * half]
        parts.append(yg * lax.rsqrt(jnp.mean(yg * yg, axis=-1, keepdims=True) + EPS))
    o_ref[...] = (jnp.concatenate(parts, axis=1) * ng_ref[...]).astype(o_ref.dtype)


def _ssd(xs, bc, dt, a, batch, n_ctx, rev, yf=None, p=None, d_skip=None, norm_g=None):
    n = xs.shape[0]
    t = CHUNK
    cpb = n // batch // t
    cc = n_ctx // t

    def chunk(b, c):
        if rev:
            c = jnp.where(c < cc, cc - 1 - c, cpb - 1 - (c - cc))
        return b * cpb + c

    row = lambda w: pl.BlockSpec((t, w), lambda b, c: (chunk(b, c), 0))
    in_specs = [row(1024), row(512), row(DT_PAD), row(DT_PAD)]
    args = [xs, bc, dt, a]
    if rev:
        vec = pl.BlockSpec((1, 1024), lambda b, c: (0, 0))
        in_specs += [row(1024), pl.BlockSpec((t, 1024), lambda b, c: (chunk(b, c), _blk("d_z", 1024))), vec, vec]
        args += [yf, p, jnp.repeat(d_skip.astype(F32), M_HEAD_DIM).reshape(1, -1), norm_g.astype(F32).reshape(1, -1)]
    return pl.pallas_call(
        functools.partial(_ssd_kernel, rev=rev), grid=(batch, cpb), in_specs=in_specs,
        out_specs=row(1024), out_shape=jax.ShapeDtypeStruct((n, 1024), CDT if rev else F32),
        scratch_shapes=[pltpu.VMEM((M_GROUPS, M_STATE, M_HPG * M_HEAD_DIM), F32)],
        compiler_params=_params(("parallel", "arbitrary")), name="ssd_rev" if rev else "ssd_fwd")(*args)


def _merge_kernel(oa_ref, ob_ref, oc_ref, od_ref, ga_ref, gb_ref, gc_ref, gd_ref,
                  wa_ref, wb_ref, wc_ref, wd_ref, wo_ref, o_ref):
    acc = None
    for o, g, w in ((oa_ref, ga_ref, wa_ref), (ob_ref, gb_ref, wb_ref), (oc_ref, gc_ref, wc_ref), (od_ref, gd_ref, wd_ref)):
        gate = 1.0 / (1.0 + jnp.exp(-g[...].astype(F32)))
        term = gate * jnp.dot(o[...], w[...], preferred_element_type=F32)
        acc = term if acc is None else acc + term
    o_ref[...] = jnp.dot(acc.astype(wo_ref.dtype), wo_ref[...], preferred_element_type=F32)


def _merge(oa, ob, oc, od, p, w_br, w_out):
    n = oa.shape[0]
    d = w_out.shape[0]
    tt = TOK_TILE
    row = lambda w: pl.BlockSpec((tt, w), lambda i: (i, 0))
    gate = lambda name: pl.BlockSpec((tt, d), lambda i: (i, _blk(name, d)))
    full = lambda r: pl.BlockSpec((r, d), lambda i: (0, 0))
    return pl.pallas_call(
        _merge_kernel, grid=(n // tt,),
        in_specs=[row(512), row(512), row(512), row(1024), gate("g_a"), gate("g_b"), gate("g_c"), gate("g_d"),
                  full(512), full(512), full(512), full(1024), full(d)],
        out_specs=row(d), out_shape=jax.ShapeDtypeStruct((n, d), F32),
        compiler_params=_params(("parallel",)), name="merge")(
            oa, ob, oc, od, p, p, p, p, w_br[0:512], w_br[512:1024], w_br[1024:1536], w_br[1536:2560], w_out)


def _silu_rows_kernel(c_ref, o_ref):
    o_ref[...] = _silu(c_ref[...]).astype(o_ref.dtype)


def _rope_tables(rows, n_ctx):
    row = jnp.repeat(jnp.arange(rows, dtype=F32), GRID_W)
    col = jnp.tile(jnp.arange(GRID_W, dtype=F32), rows)
    nq = HEAD_DIM // 4
    inv = ROPE_THETA ** (-jnp.arange(nq, dtype=F32) / nq)
    ar, ac = row[:, None] * inv, col[:, None] * inv
    ang = jnp.concatenate([ar, ar, ac, ac], axis=-1)
    ang = jnp.concatenate([jnp.zeros((n_ctx, HEAD_DIM), F32), ang], axis=0)
    return jnp.tile(jnp.cos(ang), (1, 2)), jnp.tile(jnp.sin(ang), (1, 2))


def kernel(x, c, ctx, c_ctx, w_ada, b_ada, g_norm1, g_norm2, w_in, a_sink, b_qnorm, b_knorm, c_lam_q1, c_lam_k1, c_lam_q2, c_lam_k2, c_subln, m_conv_w, m_conv_b, m_dt_bias, m_a_log, m_d, m_norm, w_br, w_out, p_wq, p_subkeys, p_u, p_v, g_final):
    batch, s, d = x.shape
    n_ctx = ctx.shape[1]
    depth = w_in.shape[0]
    assert n_ctx == TOK_TILE and s % TOK_TILE == 0 and s % GRID_W == 0
    tpb = (s + n_ctx) // TOK_TILE
    xc = jnp.concatenate([ctx, x], axis=1).reshape(-1, d)
    cos, sin = _rope_tables(s // GRID_W, n_ctx)

    cond = jnp.concatenate([c, c_ctx[None, :], jnp.zeros((16 - batch - 1, d), F32)], axis=0)
    cond = pl.pallas_call(_silu_rows_kernel, out_shape=jax.ShapeDtypeStruct(cond.shape, CDT), name="silu")(cond)

    def mod_rows(m):
        m6 = m.reshape(16, 6, d)
        lat, con = m6[:batch], jnp.broadcast_to(m6[batch][None], (batch, 6, d))
        both = jnp.stack([con, lat], axis=1).reshape(2 * batch, 6, d)
        return [both[:, k][:, None, :] for k in range(6)]

    w_proj = jnp.concatenate([w_in[:, :, _SRC[nm][0]:_SRC[nm][0] + _SRC[nm][1]] for nm in _ORDER], axis=-1).astype(CDT)
    o_dt = _SRC["d_dt"][0]
    w_dt = jnp.pad(w_in[:, :, o_dt:o_dt + 2 * M_HEADS], ((0, 0), (0, 0), (0, DT_PAD - 2 * M_HEADS))).astype(CDT)

    delta, gate_prev = None, None
    for l in range(depth):
        mods = _matmul(cond, w_ada[l].astype(CDT), F32, 16, 768, bias=b_ada[l].reshape(1, -1))
        sh1, sc1, gt1, sh2, sc2, gt2 = mod_rows(mods)
        xc, h = _norm_mod(xc, delta, gate_prev, g_norm1[l], sh1, sc1, tpb)
        tm = 2 * TOK_TILE if h.shape[0] % (2 * TOK_TILE) == 0 else TOK_TILE
        p = _matmul(h, w_proj[l], CDT, tm, 512)
        dt_raw = _matmul(h, w_dt[l], F32, tm, DT_PAD)
        q2, kv2 = _qk_prep(p, cos, sin, b_qnorm[l].astype(F32), b_knorm[l].astype(F32), tpb)
        o_a = _attn_a(q2, kv2, a_sink[l], batch, n_ctx)
        o_b = _attn_b(q2, kv2, batch, n_ctx)
        f32 = lambda v: v.astype(F32)
        lam_init = 0.8 - 0.6 * math.exp(-0.3 * l)
        lam = (jnp.exp(jnp.sum(f32(c_lam_q1[l]) * f32(c_lam_k1[l])))
               - jnp.exp(jnp.sum(f32(c_lam_q2[l]) * f32(c_lam_k2[l]))) + lam_init)
        o_c = _attn_c(q2, p, lam, c_subln[l], lam_init, batch, n_ctx)
        xs, bc, dt, a = _ssd_prep(p, dt_raw, m_conv_w[l], m_conv_b[l], m_dt_bias[l].reshape(-1), m_a_log[l].reshape(-1), tpb)
        y_f = _ssd(xs, bc, dt, a, batch, n_ctx, rev=False)
        o_d = _ssd(xs, bc, dt, a, batch, n_ctx, rev=True, yf=y_f, p=p, d_skip=m_d[l], norm_g=m_norm[l])
        mix = _merge(o_a, o_b, o_c, o_d, p, w_br[l].astype(CDT), w_out[l].astype(CDT))
        xc, _, ht = _norm_mod(xc, mix, gt1, g_norm2[l], sh2, sc2, tpb, want_t=True)
        delta = _peer(ht, p_wq[l].T.astype(CDT), p_subkeys[l].astype(CDT), p_u[l].astype(CDT), p_v[l].T.astype(CDT))
        gate_prev = gt2
    zero = jnp.zeros((2 * batch, 1, d), F32)
    _, y = _norm_mod(xc, delta, gate_prev, g_final, zero, zero, tpb, out_dtype=F32)
    return y.reshape(batch, n_ctx + s, d)[:, n_ctx:]
```

```python
import functools
import math

import jax
import jax.numpy as jnp
from jax import lax
from jax.experimental import pallas as pl
from jax.experimental.pallas import tpu as pltpu

F32 = jnp.float32
CDT = jnp.bfloat16

EPS = 1e-6
NEG = -1e30
HEAD_DIM = 64
GRID_W = 64
ROPE_THETA = 10000.0
WINDOW = 128
A_HEADS, A_KV = 8, 2
B_HEADS, B_KV = 8, 2
C_HEADS = 4
M_HEADS, M_HEAD_DIM, M_GROUPS, M_STATE = 16, 64, 2, 128
M_INNER = M_HEADS * M_HEAD_DIM
M_HPG = M_HEADS // M_GROUPS
CHUNK = 128
P_HEADS, N_KEYS, P_TOPK = 8, 128, 16
TOK_TILE = 256

VMEM_LIMIT = 48 * 1024 * 1024


def _params(sem):
    return pltpu.CompilerParams(dimension_semantics=sem, vmem_limit_bytes=VMEM_LIMIT)


def _mm_kernel(a_ref, b_ref, o_ref):
    o_ref[...] = jnp.dot(a_ref[...], b_ref[...], preferred_element_type=F32).astype(o_ref.dtype)


def _mm_bias_kernel(a_ref, b_ref, bias_ref, o_ref):
    acc = jnp.dot(a_ref[...], b_ref[...], preferred_element_type=F32)
    o_ref[...] = (acc + bias_ref[...]).astype(o_ref.dtype)


def _matmul(a, b, out_dtype, tm, tn, bias=None):
    m, k = a.shape
    n = b.shape[1]
    assert m % tm == 0 and n % tn == 0, (a.shape, b.shape, tm, tn)
    in_specs = [pl.BlockSpec((tm, k), lambda i, j: (i, 0)),
                pl.BlockSpec((k, tn), lambda i, j: (0, j))]
    args = [a, b]
    kern = _mm_kernel
    if bias is not None:
        in_specs.append(pl.BlockSpec((1, tn), lambda i, j: (0, j)))
        args.append(bias)
        kern = _mm_bias_kernel
    return pl.pallas_call(
        kern, grid=(m // tm, n // tn), in_specs=in_specs,
        out_specs=pl.BlockSpec((tm, tn), lambda i, j: (i, j)),
        out_shape=jax.ShapeDtypeStruct((m, n), out_dtype),
        compiler_params=_params(("parallel", "parallel")), name="matmul")(*args)


def _norm_mod_kernel(*refs, has_delta, want_t):
    it = iter(refs)
    x_ref = next(it)
    if has_delta:
        d_ref, gp_ref = next(it), next(it)
    g_ref, sh_ref, sc_ref = next(it), next(it), next(it)
    if has_delta:
        xo_ref = next(it)
    h_ref = next(it)
    x = x_ref[...]
    if has_delta:
        x = x + gp_ref[0] * d_ref[...]
        xo_ref[...] = x
    y = x * lax.rsqrt(jnp.mean(x * x, axis=-1, keepdims=True) + EPS) * g_ref[...]
    h = y * (1.0 + sc_ref[0]) + sh_ref[0]
    h_ref[...] = h.astype(h_ref.dtype)
    if want_t:
        ht_ref = next(it)
        ht_ref[...] = h.T.astype(ht_ref.dtype)


def _norm_mod(x, delta, gate_prev, g, shift, scale, tiles_per_batch, want_t=False, out_dtype=None):
    n, d = x.shape
    tt = TOK_TILE
    has_delta = delta is not None
    out_dtype = out_dtype or CDT

    def mod_idx(i):
        return ((i // tiles_per_batch) * 2 + jnp.minimum(i % tiles_per_batch, 1), 0, 0)

    row = pl.BlockSpec((tt, d), lambda i: (i, 0))
    mod = pl.BlockSpec((1, 1, d), mod_idx)
    in_specs, args = [row], [x]
    if has_delta:
        in_specs += [row, mod]
        args += [delta, gate_prev]
    in_specs += [pl.BlockSpec((1, d), lambda i: (0, 0)), mod, mod]
    args += [g.reshape(1, d), shift, scale]
    out_specs, out_shape = [], []
    if has_delta:
        out_specs.append(row)
        out_shape.append(jax.ShapeDtypeStruct((n, d), F32))
    out_specs.append(row)
    out_shape.append(jax.ShapeDtypeStruct((n, d), out_dtype))
    if want_t:
        out_specs.append(pl.BlockSpec((d, tt), lambda i: (0, i)))
        out_shape.append(jax.ShapeDtypeStruct((d, n), CDT))
    outs = pl.pallas_call(
        functools.partial(_norm_mod_kernel, has_delta=has_delta, want_t=want_t),
        grid=(n // tt,), in_specs=in_specs, out_specs=out_specs, out_shape=out_shape,
        compiler_params=_params(("parallel",)), name="norm_mod")(*args)
    outs = list(outs)
    x_new = outs.pop(0) if has_delta else x
    return (x_new, *outs)


_L2_SIZES = tuple(16 if a == 0 else 8 for a in range(P_TOPK))
_L2_VALID = tuple(P_TOPK // (a + 1) for a in range(P_TOPK))
_L2_ROWS = sum(_L2_SIZES)


def _knock_out_max(work, exact):
    m = jnp.max(work, axis=0, keepdims=True)
    hit = work == m
    if exact:
        row = lax.broadcasted_iota(jnp.int32, work.shape, 0)
        hit = row == jnp.min(jnp.where(hit, row, work.shape[0]), axis=0, keepdims=True)
    return m, hit


def _top16(s, exact):
    rank = jnp.full(s.shape, 127.0, F32)
    vals = []
    work = s
    for a in range(P_TOPK):
        m, hit = _knock_out_max(work, exact)
        rank = jnp.where(hit, float(a), rank)
        work = jnp.where(hit, -jnp.inf, work)
        vals.append(m)
    return jnp.concatenate(vals, axis=0), rank


def _route(s1, s2, exact):
    tt = s1.shape[1]
    sv1, rank1 = _top16(s1, exact)
    sv2, rank2 = _top16(s2, exact)
    pieces = []
    for a in range(P_TOPK):
        sz = _L2_SIZES[a]
        b_iota = lax.broadcasted_iota(jnp.int32, (sz, tt), 0)
        pieces.append(jnp.where(b_iota < _L2_VALID[a], sv1[a:a + 1, :] + sv2[0:sz, :], -jnp.inf))
    cand0 = jnp.concatenate(pieces, axis=0)
    best0 = sv1[0:1, :] + sv2[0:1, :]
    cand = cand0
    z = jnp.zeros((1, tt), F32)
    for _ in range(P_TOPK):
        m, hit = _knock_out_max(cand, exact)
        cand = jnp.where(hit, -jnp.inf, cand)
        z = z + jnp.exp(m - best0)
    sel = jnp.where(cand != cand0, 1.0, 0.0)
    cnt1 = jnp.zeros((N_KEYS, tt), F32)
    off = 0
    for a in range(P_TOPK):
        c_a = jnp.sum(sel[off:off + _L2_SIZES[a], :], axis=0, keepdims=True)
        cnt1 = jnp.where(rank1 == float(a), c_a, cnt1)
        off += _L2_SIZES[a]
    count = lambda r: jnp.sum(jnp.where(r < float(P_TOPK), 1.0, 0.0), axis=0, keepdims=True)
    excess = (count(rank1) + count(rank2) + jnp.sum(sel, axis=0, keepdims=True)) - 3.0 * P_TOPK
    return jnp.exp(s1 - sv1[0:1, :]), cnt1, jnp.exp(s2 - sv2[0:1, :]) / z, rank2, excess


def _route_kernel(qt_ref, sk_ref, e1_ref, c1_ref, e2_ref, r2_ref):
    s1 = jnp.dot(sk_ref[0, 0], qt_ref[0:N_KEYS, :], preferred_element_type=F32)
    s2 = jnp.dot(sk_ref[0, 1], qt_ref[N_KEYS:2 * N_KEYS, :], preferred_element_type=F32)

    def emit(exact):
        e1, cnt1, e2, rank2, excess = _route(s1, s2, exact)
        e1_ref[0] = e1
        c1_ref[0] = cnt1
        e2_ref[0] = e2.astype(e2_ref.dtype)
        r2_ref[0] = rank2.astype(r2_ref.dtype)
        return excess

    excess = emit(False)

    @pl.when(jnp.max(excess) > 0.0)
    def _():
        emit(True)


def _peer_route(qt, sub_keys, tt):
    n = qt.shape[1]
    dk = sub_keys.shape[-1]
    out = lambda dt: jax.ShapeDtypeStruct((P_HEADS, N_KEYS, n), dt)
    ospec = pl.BlockSpec((1, N_KEYS, tt), lambda i, h: (h, 0, i))
    return pl.pallas_call(
        _route_kernel, grid=(n // tt, P_HEADS),
        in_specs=[pl.BlockSpec((2 * dk, tt), lambda i, h: (h, i)),
                  pl.BlockSpec((1, 2, N_KEYS, dk), lambda i, h: (h, 0, 0, 0))],
        out_specs=[ospec] * 4, out_shape=[out(F32), out(F32), out(CDT), out(CDT)],
        compiler_params=_params(("parallel", "parallel")), name="peer_route")(qt, sub_keys)


def _gelu(x):
    return 0.5 * x * (1.0 + lax.erf(x * (2.0 ** -0.5)))


def _peer_dense_kernel(ht_ref, eu_ref, evt_ref, e1_ref, c1_ref, e2_ref, r2_ref, o_ref, acc_ref, *, rows):
    k = pl.program_id(1)

    @pl.when(k == 0)
    def _():
        acc_ref[...] = jnp.zeros_like(acc_ref)

    act = _gelu(jnp.dot(eu_ref[...], ht_ref[...], preferred_element_type=F32))
    gdt = e2_ref.dtype
    tt = act.shape[1]
    zero = jnp.zeros((), gdt)

    def sublane_bcast(row_ref, h, i):
        tile = jnp.broadcast_to(row_ref[h, pl.ds(i, 1), :], (16, tt)).astype(gdt)
        return jnp.concatenate([tile] * (N_KEYS // 16), axis=0)

    pieces = []
    for ii in range(rows):
        i = k * rows + ii
        g = None
        for h in range(P_HEADS):
            term = jnp.where(r2_ref[h] < sublane_bcast(c1_ref, h, i), e2_ref[h], zero) * sublane_bcast(e1_ref, h, i)
            g = term if g is None else g + term
        pieces.append(g)
    gate = jnp.concatenate(pieces, axis=0)
    w = (gate * act.astype(gdt)).astype(evt_ref.dtype)
    acc_ref[...] += jnp.dot(evt_ref[...], w, preferred_element_type=F32)

    @pl.when(k == pl.num_programs(1) - 1)
    def _():
        o_ref[...] = acc_ref[...].T


def _peer_dense(ht, eu, evt, e1, c1, e2, r2, tt, rows):
    d, n = ht.shape
    n_exp = eu.shape[0]
    eb = rows * N_KEYS
    rt = pl.BlockSpec((P_HEADS, N_KEYS, tt), lambda i, k: (0, 0, i))
    return pl.pallas_call(
        functools.partial(_peer_dense_kernel, rows=rows),
        grid=(n // tt, n_exp // eb),
        in_specs=[pl.BlockSpec((d, tt), lambda i, k: (0, i)),
                  pl.BlockSpec((eb, d), lambda i, k: (k, 0)),
                  pl.BlockSpec((d, eb), lambda i, k: (0, k)),
                  rt, rt, rt, rt],
        out_specs=pl.BlockSpec((tt, d), lambda i, k: (i, 0)),
        out_shape=jax.ShapeDtypeStruct((n, d), F32),
        scratch_shapes=[pltpu.VMEM((d, tt), F32)],
        compiler_params=_params(("parallel", "arbitrary")), name="peer_dense")(ht, eu, evt, e1, c1, e2, r2)


def _peer(ht, wqt, sub_keys, eu, evt):
    n = ht.shape[1]
    tt = 512 if n % 512 == 0 else 256
    qt = _matmul(wqt, ht, CDT, 512, tt)
    e1, c1, e2, r2 = _peer_route(qt, sub_keys, 256)
    return _peer_dense(ht, eu, evt, e1, c1, e2, r2, tt, rows=8)


_SRC = {}
_off = 0
for _name, _w in (("a_q", 512), ("a_k", 128), ("a_v", 128), ("b_q", 512), ("b_k", 128), ("b_v", 128),
                  ("c_q", 512), ("c_k", 512), ("c_v", 512), ("d_z", 1024), ("d_xs", 1024), ("d_b", 256),
                  ("d_c", 256), ("d_dt", 32), ("g_a", 1024), ("g_b", 1024), ("g_c", 1024), ("g_d", 1024)):
    _SRC[_name] = (_off, _w)
    _off += _w
_ORDER = ("d_z", "d_xs", "g_a", "g_b", "g_c", "g_d", "a_q", "b_q", "c_q", "c_k", "c_v",
          "d_b", "d_c", "a_k", "a_v", "b_k", "b_v")
_DST = {}
_off = 0
for _name in _ORDER:
    _DST[_name] = _off
    _off += _SRC[_name][1]
P_WIDTH = _off
DT_PAD = 128


def _blk(name, width):
    return _DST[name] // width


def _rope(x, cos, sin):
    w = x.shape[1]
    lane = lax.broadcasted_iota(jnp.int32, x.shape, 1)
    fwd = pltpu.roll(x, 16, 1)
    bwd = pltpu.roll(x, w - 16, 1)
    rot = jnp.where(lane % 32 < 16, -bwd, fwd)
    return x * cos + rot * sin


def _head_rms(x, bd):
    sq = x * x
    hi = sq.astype(jnp.bfloat16)
    lo = (sq - hi.astype(F32)).astype(jnp.bfloat16)
    ms = jnp.dot(hi, bd, preferred_element_type=F32) + jnp.dot(lo, bd, preferred_element_type=F32)
    return lax.rsqrt(ms + EPS)


def _qk_prep_kernel(q_ref, kv_ref, cs_ref, sn_ref, qg_ref, kg_ref, bd_ref, qo_ref, kvo_ref):
    cos2, sin2 = cs_ref[...], sn_ref[...]
    cos4 = jnp.concatenate([cos2] * 4, axis=1)
    sin4 = jnp.concatenate([sin2] * 4, axis=1)
    scale = HEAD_DIM ** -0.5
    bd = bd_ref[...]
    a_q = q_ref[:, 0:512].astype(F32)
    qo_ref[:, 0:512] = (_rope(a_q, cos4, sin4) * scale).astype(qo_ref.dtype)
    b_q = q_ref[:, 512:1024].astype(F32)
    b_q = b_q * _head_rms(b_q, bd) * qg_ref[...]
    qo_ref[:, 512:1024] = (_rope(b_q, cos4, sin4) * scale).astype(qo_ref.dtype)
    c_q = q_ref[:, 1024:1536].astype(F32)
    qo_ref[:, 1024:1536] = (_rope(c_q, cos4, sin4) * scale).astype(qo_ref.dtype)
    c_k = q_ref[:, 1536:2048].astype(F32)
    qo_ref[:, 1536:2048] = _rope(c_k, cos4, sin4).astype(qo_ref.dtype)
    a_k = kv_ref[:, 0:128].astype(F32)
    kvo_ref[:, 0:128] = _rope(a_k, cos2, sin2).astype(kvo_ref.dtype)
    kvo_ref[:, 128:256] = kv_ref[:, 128:256]
    b_k = kv_ref[:, 256:384].astype(F32)
    b_k = b_k * _head_rms(b_k, bd[0:128, 0:128]) * kg_ref[...]
    kvo_ref[:, 256:384] = _rope(b_k, cos2, sin2).astype(kvo_ref.dtype)
    kvo_ref[:, 384:512] = kv_ref[:, 384:512]


def _qk_prep(p, cos, sin, qn_g, kn_g, tiles_per_batch):
    n = p.shape[0]
    tt = TOK_TILE
    lane = jnp.arange(512)
    bd = jnp.where(lane[:, None] // HEAD_DIM == lane[None, :] // HEAD_DIM, 1.0 / HEAD_DIM, 0.0).astype(jnp.bfloat16)
    pos = pl.BlockSpec((tt, 128), lambda i: (i % tiles_per_batch, 0))
    full = lambda r, c: pl.BlockSpec((r, c), lambda i: (0, 0))
    return pl.pallas_call(
        _qk_prep_kernel, grid=(n // tt,),
        in_specs=[pl.BlockSpec((tt, 2048), lambda i: (i, _blk("a_q", 2048))),
                  pl.BlockSpec((tt, 512), lambda i: (i, _blk("a_k", 512))),
                  pos, pos, full(1, 512), full(1, 128), full(512, 512)],
        out_specs=[pl.BlockSpec((tt, 2048), lambda i: (i, 0)), pl.BlockSpec((tt, 512), lambda i: (i, 0))],
        out_shape=[jax.ShapeDtypeStruct((n, 2048), CDT), jax.ShapeDtypeStruct((n, 512), CDT)],
        compiler_params=_params(("parallel",)), name="qk_prep")(
            p, p, cos, sin, jnp.tile(qn_g, 8).reshape(1, 512), jnp.tile(kn_g, 2).reshape(1, 128), bd)


_NT = (((1,), (1,)), ((), ()))


def _stack_heads(ref, first, count):
    return jnp.concatenate([ref[:, (first + i) * HEAD_DIM:(first + i + 1) * HEAD_DIM] for i in range(count)], axis=0)


def _attn_a_kernel(q_ref, k_ref, v_ref, sink_ref, o_ref, *, n_ctx):
    tq = q_ref.shape[0]
    total = k_ref.shape[0]
    t = pl.program_id(1)
    n = t - n_ctx // tq
    kstart = pl.multiple_of(jnp.clip(n_ctx + (n - 1) * tq, n_ctx - tq, total - 3 * tq), tq)
    qpos = n * tq + lax.broadcasted_iota(jnp.int32, (tq, 3 * tq), 0)
    kpos = kstart - n_ctx + lax.broadcasted_iota(jnp.int32, (tq, 3 * tq), 1)
    valid = (jnp.abs(qpos - kpos) <= WINDOW) & (kpos >= 0) & (n >= 0)
    groups = A_HEADS // A_KV
    for g in range(A_KV):
        q = _stack_heads(q_ref, g * groups, groups)
        hs = slice(g * HEAD_DIM, (g + 1) * HEAD_DIM)
        kc, vc = k_ref[0:n_ctx, hs], v_ref[0:n_ctx, hs]
        kl, vl = k_ref[pl.ds(kstart, 3 * tq), hs], v_ref[pl.ds(kstart, 3 * tq), hs]
        sc = lax.dot_general(q, kc, _NT, preferred_element_type=F32)
        sl = lax.dot_general(q, kl, _NT, preferred_element_type=F32)
        for i in range(groups):
            rows = slice(i * tq, (i + 1) * tq)
            hq = g * groups + i
            sink = sink_ref[hq:hq + 1, 0:1]
            sci = sc[rows]
            sli = jnp.where(valid, sl[rows], NEG)
            m = jnp.maximum(jnp.maximum(jnp.max(sci, axis=1, keepdims=True), jnp.max(sli, axis=1, keepdims=True)), sink)
            ec = jnp.exp(sci - m)
            el = jnp.exp(sli - m)
            den = jnp.sum(ec, axis=1, keepdims=True) + jnp.sum(el, axis=1, keepdims=True) + jnp.exp(sink - m)
            o = jnp.dot(ec.astype(vc.dtype), vc, preferred_element_type=F32)
            o = o + jnp.dot(el.astype(vl.dtype), vl, preferred_element_type=F32)
            o_ref[:, hq * HEAD_DIM:(hq + 1) * HEAD_DIM] = (o / den).astype(o_ref.dtype)


def _attn_a(q2, kv2, sink, batch, n_ctx):
    n = q2.shape[0]
    total = n // batch
    tq = WINDOW
    tpb = total // tq
    kvspec = lambda c: pl.BlockSpec((total, 128), lambda b, t: (b, c))
    return pl.pallas_call(
        functools.partial(_attn_a_kernel, n_ctx=n_ctx), grid=(batch, tpb),
        in_specs=[pl.BlockSpec((tq, 512), lambda b, t: (b * tpb + t, 0)), kvspec(0), kvspec(1),
                  pl.BlockSpec((A_HEADS, 128), lambda b, t: (0, 0))],
        out_specs=pl.BlockSpec((tq, 512), lambda b, t: (b * tpb + t, 0)),
        out_shape=jax.ShapeDtypeStruct((n, 512), CDT),
        compiler_params=_params(("parallel", "parallel")), name="attn_a")(
            q2, kv2, kv2, jnp.broadcast_to(sink.astype(F32)[:, None], (A_HEADS, 128)))


def _softmax_attend(q, k, v):
    s = lax.dot_general(q, k, _NT, preferred_element_type=F32)
    p = jnp.exp(s - jnp.max(s, axis=1, keepdims=True))
    l = jnp.sum(p, axis=1, keepdims=True)
    return jnp.dot(p.astype(v.dtype), v, preferred_element_type=F32) / l


def _on_key_range(k_ref, n_ctx, run):
    @pl.when(pl.program_id(1) == 0)
    def _():
        run(n_ctx)

    @pl.when(pl.program_id(1) != 0)
    def _():
        run(k_ref.shape[0])


def _attn_b_kernel(q_ref, k_ref, v_ref, o_ref, *, n_ctx):
    tq = q_ref.shape[0]
    groups = B_HEADS // B_KV
    stack = 2

    def run(nkeys):
        for g in range(B_KV):
            hs = slice(g * HEAD_DIM, (g + 1) * HEAD_DIM)
            k, v = k_ref[0:nkeys, hs], v_ref[0:nkeys, hs]
            for j in range(groups // stack):
                first = g * groups + j * stack
                o = _softmax_attend(_stack_heads(q_ref, first, stack), k, v)
                for i in range(stack):
                    hq = first + i
                    o_ref[:, hq * HEAD_DIM:(hq + 1) * HEAD_DIM] = o[i * tq:(i + 1) * tq].astype(o_ref.dtype)

    _on_key_range(k_ref, n_ctx, run)


def _attn_b(q2, kv2, batch, n_ctx):
    n = q2.shape[0]
    total = n // batch
    tq = TOK_TILE
    tpb = total // tq
    kvspec = lambda c: pl.BlockSpec((total, 128), lambda b, t: (b, c))
    return pl.pallas_call(
        functools.partial(_attn_b_kernel, n_ctx=n_ctx), grid=(batch, tpb),
        in_specs=[pl.BlockSpec((tq, 512), lambda b, t: (b * tpb + t, 1)), kvspec(2), kvspec(3)],
        out_specs=pl.BlockSpec((tq, 512), lambda b, t: (b * tpb + t, 0)),
        out_shape=jax.ShapeDtypeStruct((n, 512), CDT),
        compiler_params=_params(("parallel", "parallel")), name="attn_b")(q2, kv2, kv2)


def _attn_c_kernel(q_ref, k_ref, v_ref, lam_ref, g_ref, o_ref, *, n_ctx, out_scale):
    lam = lam_ref[...]

    def run(nkeys):
        for h in range(C_HEADS):
            c0 = 2 * h * HEAD_DIM
            v = v_ref[0:nkeys, c0:c0 + 2 * HEAD_DIM]
            maps = []
            for t in range(2):
                cols = slice(c0 + t * HEAD_DIM, c0 + (t + 1) * HEAD_DIM)
                maps.append(_softmax_attend(q_ref[:, cols], k_ref[0:nkeys, cols], v))
            o = maps[0] - lam * maps[1]
            o = o * lax.rsqrt(jnp.mean(o * o, axis=-1, keepdims=True) + EPS) * g_ref[...] * out_scale
            o_ref[:, c0:c0 + 2 * HEAD_DIM] = o.astype(o_ref.dtype)

    _on_key_range(k_ref, n_ctx, run)


def _attn_c(q2, p, lam, subln_g, lam_init, batch, n_ctx):
    n = q2.shape[0]
    total = n // batch
    tq = TOK_TILE
    tpb = total // tq
    vec = pl.BlockSpec((1, 128), lambda b, t: (0, 0))
    return pl.pallas_call(
        functools.partial(_attn_c_kernel, n_ctx=n_ctx, out_scale=1.0 - lam_init), grid=(batch, tpb),
        in_specs=[pl.BlockSpec((tq, 512), lambda b, t: (b * tpb + t, 2)),
                  pl.BlockSpec((total, 512), lambda b, t: (b, 3)),
                  pl.BlockSpec((total, 512), lambda b, t: (b, _blk("c_v", 512))), vec, vec],
        out_specs=pl.BlockSpec((tq, 512), lambda b, t: (b * tpb + t, 0)),
        out_shape=jax.ShapeDtypeStruct((n, 512), CDT),
        compiler_params=_params(("parallel", "parallel")), name="attn_c")(
            q2, q2, p, jnp.broadcast_to(lam.astype(F32), (1, 128)), subln_g.astype(F32).reshape(1, 128))


def _silu(x):
    return x / (1.0 + jnp.exp(-x))


def _conv3(cur, prev_row, next_row, w_ref, b_ref):
    t = cur.shape[0]
    row = lax.broadcasted_iota(jnp.int32, cur.shape, 0)
    up = jnp.where(row == 0, prev_row, pltpu.roll(cur, 1, 0))
    dn = jnp.where(row == t - 1, next_row, pltpu.roll(cur, t - 1, 0))
    return up * w_ref[0:1, :] + cur * w_ref[1:2, :] + dn * w_ref[2:3, :] + b_ref[...]


def _ssd_prep_kernel(xs_ref, xsp_ref, xsn_ref, bc_ref, bcp_ref, bcn_ref, dt_ref, wx_ref, bx_ref, wbc_ref, bbc_ref,
                     dtb_ref, a_ref, xo_ref, bco_ref, dto_ref, ao_ref, *, tiles_per_batch):
    tb = pl.program_id(0) % tiles_per_batch
    has_prev = (tb >= 2).astype(F32)
    has_next = ((tb != 0) & (tb != tiles_per_batch - 1)).astype(F32)
    hl = xsp_ref.shape[0]
    xs = _conv3(xs_ref[...].astype(F32), xsp_ref[hl - 1:hl, :].astype(F32) * has_prev,
                xsn_ref[0:1, :].astype(F32) * has_next, wx_ref, bx_ref)
    xo_ref[...] = _silu(xs)
    bc = _conv3(bc_ref[...].astype(F32), bcp_ref[hl - 1:hl, :].astype(F32) * has_prev,
                bcn_ref[0:1, :].astype(F32) * has_next, wbc_ref, bbc_ref)
    bco_ref[...] = _silu(bc).astype(bco_ref.dtype)
    u = dt_ref[...] + dtb_ref[...]
    dt = jnp.maximum(u, 0.0) + jnp.log1p(jnp.exp(-jnp.abs(u)))
    dto_ref[...] = dt
    ao_ref[...] = dt * a_ref[...]


def _ssd_prep(p, dt_raw, conv_w, conv_b, dt_bias, a_log, tiles_per_batch):
    n = p.shape[0]
    tt = TOK_TILE
    halo = 16
    per = tt // halo
    last = n // halo - 1
    cur = lambda name, w: pl.BlockSpec((tt, w), lambda i: (i, _blk(name, w)))
    prv = lambda name, w: pl.BlockSpec((halo, w), lambda i: (jnp.maximum(i * per - 1, 0), _blk(name, w)))
    nxt = lambda name, w: pl.BlockSpec((halo, w), lambda i: (jnp.minimum((i + 1) * per, last), _blk(name, w)))
    full = lambda r, c: pl.BlockSpec((r, c), lambda i: (0, 0))
    pad = lambda v: jnp.pad(v.astype(F32).reshape(1, -1), ((0, 0), (0, DT_PAD - 2 * M_HEADS)))
    cw, cb = conv_w.astype(F32), conv_b.astype(F32).reshape(1, -1)
    row = lambda w: pl.BlockSpec((tt, w), lambda i: (i, 0))
    return pl.pallas_call(
        functools.partial(_ssd_prep_kernel, tiles_per_batch=tiles_per_batch), grid=(n // tt,),
        in_specs=[cur("d_xs", 1024), prv("d_xs", 1024), nxt("d_xs", 1024),
                  cur("d_b", 512), prv("d_b", 512), nxt("d_b", 512), row(DT_PAD),
                  full(3, 1024), full(1, 1024), full(3, 512), full(1, 512), full(1, DT_PAD), full(1, DT_PAD)],
        out_specs=[row(1024), row(512), row(DT_PAD), row(DT_PAD)],
        out_shape=[jax.ShapeDtypeStruct((n, 1024), F32), jax.ShapeDtypeStruct((n, 512), CDT),
                   jax.ShapeDtypeStruct((n, DT_PAD), F32), jax.ShapeDtypeStruct((n, DT_PAD), F32)],
        compiler_params=_params(("parallel",)), name="ssd_prep")(
            p, p, p, p, p, p, dt_raw, cw[:, :M_INNER], cb[:, :M_INNER], cw[:, M_INNER:], cb[:, M_INNER:],
            pad(dt_bias), pad(-jnp.exp(a_log.astype(F32))))


def _split3(x):
    hi = x.astype(jnp.bfloat16)
    r = x - hi.astype(F32)
    mid = r.astype(jnp.bfloat16)
    lo = (r - mid.astype(F32)).astype(jnp.bfloat16)
    return hi, mid, lo


def _expand_heads(v, cols, rows):
    return jnp.concatenate([jnp.broadcast_to(v[:, c:c + 1], (rows, M_HEAD_DIM)) for c in cols], axis=1)


def _ssd_kernel(*refs, rev):
    if rev:
        xs_ref, bc_ref, dt_ref, a_ref, yf_ref, z_ref, dsk_ref, ng_ref, o_ref, st_ref = refs
    else:
        xs_ref, bc_ref, dt_ref, a_ref, o_ref, st_ref = refs
    t = xs_ref.shape[0]

    @pl.when(pl.program_id(1) == 0)
    def _():
        st_ref[...] = jnp.zeros_like(st_ref)

    li = lax.broadcasted_iota(jnp.int32, (t, t), 0)
    si = lax.broadcasted_iota(jnp.int32, (t, t), 1)
    tri = (si >= li) if rev else (li >= si)
    tri_b = jnp.where(tri, 1.0, 0.0).astype(jnp.bfloat16)
    cs = sum(jnp.dot(tri_b, part, preferred_element_type=F32) for part in _split3(a_ref[...]))
    cst = cs.T
    tot = cs[0:1, :] if rev else cs[t - 1:t, :]
    outdec, indec, sdec = jnp.exp(cs), jnp.exp(tot - cs), jnp.exp(tot)
    dt = dt_ref[...]
    d0 = M_HEADS if rev else 0
    ys = []
    for g in range(M_GROUPS):
        cols = [d0 + g * M_HPG + e for e in range(M_HPG)]
        lanes = slice(g * M_HPG * M_HEAD_DIM, (g + 1) * M_HPG * M_HEAD_DIM)
        bg = bc_ref[:, g * M_STATE:(g + 1) * M_STATE]
        cg = bc_ref[:, (M_GROUPS + g) * M_STATE:(M_GROUPS + g + 1) * M_STATE]
        cb = lax.dot_general(cg, bg, _NT, preferred_element_type=F32)
        bgt = bg.astype(F32).T.astype(bg.dtype)
        x = xs_ref[:, lanes] * _expand_heads(dt, cols, t)
        xb = x.astype(bg.dtype)
        hst = st_ref[g]
        y_off = jnp.dot(cg, hst.astype(cg.dtype), preferred_element_type=F32) * _expand_heads(outdec, cols, t)
        y_diag = []
        for e in range(M_HPG):
            c = cols[e]
            lm = jnp.exp(jnp.where(tri, cs[:, c:c + 1] - cst[c:c + 1, :], NEG))
            y_diag.append(jnp.dot((cb * lm).astype(xb.dtype), xb[:, e * M_HEAD_DIM:(e + 1) * M_HEAD_DIM],
                                  preferred_element_type=F32))
        ys.append(y_off + jnp.concatenate(y_diag, axis=1))
        xin = (x * _expand_heads(indec, cols, t)).astype(bg.dtype)
        st_ref[g] = hst * _expand_heads(sdec, cols, 1) + jnp.dot(bgt, xin, preferred_element_type=F32)
    y = jnp.concatenate(ys, axis=1)
    if not rev:
        o_ref[...] = y
        return
    y = (y + yf_ref[...] + xs_ref[...] * dsk_ref[...]) * _silu(z_ref[...].astype(F32))
    half = M_INNER // M_GROUPS
    parts = []
    for g in range(M_GROUPS):
        yg = y[:, g * half:(g + 1) * half]
        parts.append(yg * lax.rsqrt(jnp.mean(yg * yg, axis=-1, keepdims=True) + EPS))
    o_ref[...] = (jnp.concatenate(parts, axis=1) * ng_ref[...]).astype(o_ref.dtype)


def _ssd(xs, bc, dt, a, batch, n_ctx, rev, yf=None, p=None, d_skip=None, norm_g=None):
    n = xs.shape[0]
    t = CHUNK
    cpb = n // batch // t
    cc = n_ctx // t

    def chunk(b, c):
        if rev:
            c = jnp.where(c < cc, cc - 1 - c, cpb - 1 - (c - cc))
        return b * cpb + c

    row = lambda w: pl.BlockSpec((t, w), lambda b, c: (chunk(b, c), 0))
    in_specs = [row(1024), row(512), row(DT_PAD), row(DT_PAD)]
    args = [xs, bc, dt, a]
    if rev:
        vec = pl.BlockSpec((1, 1024), lambda b, c: (0, 0))
        in_specs += [row(1024), pl.BlockSpec((t, 1024), lambda b, c: (chunk(b, c), _blk("d_z", 1024))), vec, vec]
        args += [yf, p, jnp.repeat(d_skip.astype(F32), M_HEAD_DIM).reshape(1, -1), norm_g.astype(F32).reshape(1, -1)]
    return pl.pallas_call(
        functools.partial(_ssd_kernel, rev=rev), grid=(batch, cpb), in_specs=in_specs,
        out_specs=row(1024), out_shape=jax.ShapeDtypeStruct((n, 1024), CDT if rev else F32),
        scratch_shapes=[pltpu.VMEM((M_GROUPS, M_STATE, M_HPG * M_HEAD_DIM), F32)],
        compiler_params=_params(("parallel", "arbitrary")), name="ssd_rev" if rev else "ssd_fwd")(*args)


def _merge_kernel(oa_ref, ob_ref, oc_ref, od_ref, ga_ref, gb_ref, gc_ref, gd_ref,
                  wa_ref, wb_ref, wc_ref, wd_ref, wo_ref, o_ref):
    acc = None
    for o, g, w in ((oa_ref, ga_ref, wa_ref), (ob_ref, gb_ref, wb_ref), (oc_ref, gc_ref, wc_ref), (od_ref, gd_ref, wd_ref)):
        gate = 1.0 / (1.0 + jnp.exp(-g[...].astype(F32)))
        term = gate * jnp.dot(o[...], w[...], preferred_element_type=F32)
        acc = term if acc is None else acc + term
    o_ref[...] = jnp.dot(acc.astype(wo_ref.dtype), wo_ref[...], preferred_element_type=F32)


def _merge(oa, ob, oc, od, p, w_br, w_out):
    n = oa.shape[0]
    d = w_out.shape[0]
    tt = TOK_TILE
    row = lambda w: pl.BlockSpec((tt, w), lambda i: (i, 0))
    gate = lambda name: pl.BlockSpec((tt, d), lambda i: (i, _blk(name, d)))
    full = lambda r: pl.BlockSpec((r, d), lambda i: (0, 0))
    return pl.pallas_call(
        _merge_kernel, grid=(n // tt,),
        in_specs=[row(512), row(512), row(512), row(1024), gate("g_a"), gate("g_b"), gate("g_c"), gate("g_d"),
                  full(512), full(512), full(512), full(1024), full(d)],
        out_specs=row(d), out_shape=jax.ShapeDtypeStruct((n, d), F32),
        compiler_params=_params(("parallel",)), name="merge")(
            oa, ob, oc, od, p, p, p, p, w_br[0:512], w_br[512:1024], w_br[1024:1536], w_br[1536:2560], w_out)


def _silu_rows_kernel(c_ref, o_ref):
    o_ref[...] = _silu(c_ref[...]).astype(o_ref.dtype)


def _rope_tables(rows, n_ctx):
    row = jnp.repeat(jnp.arange(rows, dtype=F32), GRID_W)
    col = jnp.tile(jnp.arange(GRID_W, dtype=F32), rows)
    nq = HEAD_DIM // 4
    inv = ROPE_THETA ** (-jnp.arange(nq, dtype=F32) / nq)
    ar, ac = row[:, None] * inv, col[:, None] * inv
    ang = jnp.concatenate([ar, ar, ac, ac], axis=-1)
    ang = jnp.concatenate([jnp.zeros((n_ctx, HEAD_DIM), F32), ang], axis=0)
    return jnp.tile(jnp.cos(ang), (1, 2)), jnp.tile(jnp.sin(ang), (1, 2))


def kernel(x, c, ctx, c_ctx, w_ada, b_ada, g_norm1, g_norm2, w_in, a_sink, b_qnorm, b_knorm, c_lam_q1, c_lam_k1, c_lam_q2, c_lam_k2, c_subln, m_conv_w, m_conv_b, m_dt_bias, m_a_log, m_d, m_norm, w_br, w_out, p_wq, p_subkeys, p_u, p_v, g_final):
    batch, s, d = x.shape
    n_ctx = ctx.shape[1]
    depth = w_in.shape[0]
    assert n_ctx == TOK_TILE and s % TOK_TILE == 0 and s % GRID_W == 0
    tpb = (s + n_ctx) // TOK_TILE
    xc = jnp.concatenate([ctx, x], axis=1).reshape(-1, d)
    cos, sin = _rope_tables(s // GRID_W, n_ctx)

    cond = jnp.concatenate([c, c_ctx[None, :], jnp.zeros((16 - batch - 1, d), F32)], axis=0)
    cond = pl.pallas_call(_silu_rows_kernel, out_shape=jax.ShapeDtypeStruct(cond.shape, CDT), name="silu")(cond)

    def mod_rows(m):
        m6 = m.reshape(16, 6, d)
        lat, con = m6[:batch], jnp.broadcast_to(m6[batch][None], (batch, 6, d))
        both = jnp.stack([con, lat], axis=1).reshape(2 * batch, 6, d)
        return [both[:, k][:, None, :] for k in range(6)]

    w_proj = jnp.concatenate([w_in[:, :, _SRC[nm][0]:_SRC[nm][0] + _SRC[nm][1]] for nm in _ORDER], axis=-1).astype(CDT)
    o_dt = _SRC["d_dt"][0]
    w_dt = jnp.pad(w_in[:, :, o_dt:o_dt + 2 * M_HEADS], ((0, 0), (0, 0), (0, DT_PAD - 2 * M_HEADS))).astype(CDT)

    delta, gate_prev = None, None
    for l in range(depth):
        mods = _matmul(cond, w_ada[l].astype(CDT), F32, 16, 768, bias=b_ada[l].reshape(1, -1))
        sh1, sc1, gt1, sh2, sc2, gt2 = mod_rows(mods)
        xc, h = _norm_mod(xc, delta, gate_prev, g_norm1[l], sh1, sc1, tpb)
        tm = next(t for t in (8 * TOK_TILE, 2 * TOK_TILE, TOK_TILE) if h.shape[0] % t == 0)
        p = _matmul(h, w_proj[l], CDT, tm, 512)
        dt_raw = _matmul(h, w_dt[l], F32, tm, DT_PAD)
        q2, kv2 = _qk_prep(p, cos, sin, b_qnorm[l].astype(F32), b_knorm[l].astype(F32), tpb)
        o_a = _attn_a(q2, kv2, a_sink[l], batch, n_ctx)
        o_b = _attn_b(q2, kv2, batch, n_ctx)
        f32 = lambda v: v.astype(F32)
        lam_init = 0.8 - 0.6 * math.exp(-0.3 * l)
        lam = (jnp.exp(jnp.sum(f32(c_lam_q1[l]) * f32(c_lam_k1[l])))
               - jnp.exp(jnp.sum(f32(c_lam_q2[l]) * f32(c_lam_k2[l]))) + lam_init)
        o_c = _attn_c(q2, p, lam, c_subln[l], lam_init, batch, n_ctx)
        xs, bc, dt, a = _ssd_prep(p, dt_raw, m_conv_w[l], m_conv_b[l], m_dt_bias[l].reshape(-1), m_a_log[l].reshape(-1), tpb)
        y_f = _ssd(xs, bc, dt, a, batch, n_ctx, rev=False)
        o_d = _ssd(xs, bc, dt, a, batch, n_ctx, rev=True, yf=y_f, p=p, d_skip=m_d[l], norm_g=m_norm[l])
        mix = _merge(o_a, o_b, o_c, o_d, p, w_br[l].astype(CDT), w_out[l].astype(CDT))
        xc, _, ht = _norm_mod(xc, mix, gt1, g_norm2[l], sh2, sc2, tpb, want_t=True)
        delta = _peer(ht, p_wq[l].T.astype(CDT), p_subkeys[l].astype(CDT), p_u[l].astype(CDT), p_v[l].T.astype(CDT))
        gate_prev = gt2
    zero = jnp.zeros((2 * batch, 1, d), F32)
    _, y = _norm_mod(xc, delta, gate_prev, g_final, zero, zero, tpb, out_dtype=F32)
    return y.reshape(batch, n_ctx + s, d)[:, n_ctx:]
```

```python
import functools
import math

import jax
import jax.numpy as jnp
from jax import lax
from jax.experimental import pallas as pl
from jax.experimental.pallas import tpu as pltpu

F32 = jnp.float32
CDT = jnp.bfloat16

EPS = 1e-6
NEG = -1e30
HEAD_DIM = 64
GRID_W = 64
ROPE_THETA = 10000.0
WINDOW = 128
A_HEADS, A_KV = 8, 2
B_HEADS, B_KV = 8, 2
C_HEADS = 4
M_HEADS, M_HEAD_DIM, M_GROUPS, M_STATE = 16, 64, 2, 128
M_INNER = M_HEADS * M_HEAD_DIM
M_HPG = M_HEADS // M_GROUPS
CHUNK = 128
P_HEADS, N_KEYS, P_TOPK = 8, 128, 16
TOK_TILE = 256

VMEM_LIMIT = 48 * 1024 * 1024


def _params(sem):
    return pltpu.CompilerParams(dimension_semantics=sem, vmem_limit_bytes=VMEM_LIMIT)


def _mm_kernel(a_ref, b_ref, o_ref):
    o_ref[...] = jnp.dot(a_ref[...], b_ref[...], preferred_element_type=F32).astype(o_ref.dtype)


def _mm_bias_kernel(a_ref, b_ref, bias_ref, o_ref):
    acc = jnp.dot(a_ref[...], b_ref[...], preferred_element_type=F32)
    o_ref[...] = (acc + bias_ref[...]).astype(o_ref.dtype)


def _matmul(a, b, out_dtype, tm, tn, bias=None):
    m, k = a.shape
    n = b.shape[1]
    assert m % tm == 0 and n % tn == 0, (a.shape, b.shape, tm, tn)
    in_specs = [pl.BlockSpec((tm, k), lambda i, j: (i, 0)),
                pl.BlockSpec((k, tn), lambda i, j: (0, j))]
    args = [a, b]
    kern = _mm_kernel
    if bias is not None:
        in_specs.append(pl.BlockSpec((1, tn), lambda i, j: (0, j)))
        args.append(bias)
        kern = _mm_bias_kernel
    return pl.pallas_call(
        kern, grid=(m // tm, n // tn), in_specs=in_specs,
        out_specs=pl.BlockSpec((tm, tn), lambda i, j: (i, j)),
        out_shape=jax.ShapeDtypeStruct((m, n), out_dtype),
        compiler_params=_params(("parallel", "parallel")), name="matmul")(*args)


def _norm_mod_kernel(*refs, has_delta, want_t):
    it = iter(refs)
    x_ref = next(it)
    if has_delta:
        d_ref, gp_ref = next(it), next(it)
    g_ref, sh_ref, sc_ref = next(it), next(it), next(it)
    if has_delta:
        xo_ref = next(it)
    h_ref = next(it)
    x = x_ref[...]
    if has_delta:
        x = x + gp_ref[0] * d_ref[...]
        xo_ref[...] = x
    y = x * lax.rsqrt(jnp.mean(x * x, axis=-1, keepdims=True) + EPS) * g_ref[...]
    h = y * (1.0 + sc_ref[0]) + sh_ref[0]
    h_ref[...] = h.astype(h_ref.dtype)
    if want_t:
        ht_ref = next(it)
        ht_ref[...] = h.T.astype(ht_ref.dtype)


def _norm_mod(x, delta, gate_prev, g, shift, scale, tiles_per_batch, want_t=False, out_dtype=None):
    n, d = x.shape
    tt = TOK_TILE
    has_delta = delta is not None
    out_dtype = out_dtype or CDT

    def mod_idx(i):
        return ((i // tiles_per_batch) * 2 + jnp.minimum(i % tiles_per_batch, 1), 0, 0)

    row = pl.BlockSpec((tt, d), lambda i: (i, 0))
    mod = pl.BlockSpec((1, 1, d), mod_idx)
    in_specs, args = [row], [x]
    if has_delta:
        in_specs += [row, mod]
        args += [delta, gate_prev]
    in_specs += [pl.BlockSpec((1, d), lambda i: (0, 0)), mod, mod]
    args += [g.reshape(1, d), shift, scale]
    out_specs, out_shape = [], []
    if has_delta:
        out_specs.append(row)
        out_shape.append(jax.ShapeDtypeStruct((n, d), F32))
    out_specs.append(row)
    out_shape.append(jax.ShapeDtypeStruct((n, d), out_dtype))
    if want_t:
        out_specs.append(pl.BlockSpec((d, tt), lambda i: (0, i)))
        out_shape.append(jax.ShapeDtypeStruct((d, n), CDT))
    outs = pl.pallas_call(
        functools.partial(_norm_mod_kernel, has_delta=has_delta, want_t=want_t),
        grid=(n // tt,), in_specs=in_specs, out_specs=out_specs, out_shape=out_shape,
        compiler_params=_params(("parallel",)), name="norm_mod")(*args)
    outs = list(outs)
    x_new = outs.pop(0) if has_delta else x
    return (x_new, *outs)


_L2_SIZES = tuple(16 if a == 0 else 8 for a in range(P_TOPK))
_L2_VALID = tuple(P_TOPK // (a + 1) for a in range(P_TOPK))
_L2_ROWS = sum(_L2_SIZES)


def _knock_out_max(work, exact):
    m = jnp.max(work, axis=0, keepdims=True)
    hit = work == m
    if exact:
        row = lax.broadcasted_iota(jnp.int32, work.shape, 0)
        hit = row == jnp.min(jnp.where(hit, row, work.shape[0]), axis=0, keepdims=True)
    return m, hit


def _top16(s, exact):
    rank = jnp.full(s.shape, 127.0, F32)
    vals = []
    work = s
    for a in range(P_TOPK):
        m, hit = _knock_out_max(work, exact)
        rank = jnp.where(hit, float(a), rank)
        work = jnp.where(hit, -jnp.inf, work)
        vals.append(m)
    return jnp.concatenate(vals, axis=0), rank


def _route(s1, s2, exact):
    tt = s1.shape[1]
    sv1, rank1 = _top16(s1, exact)
    sv2, rank2 = _top16(s2, exact)
    pieces = []
    for a in range(P_TOPK):
        sz = _L2_SIZES[a]
        b_iota = lax.broadcasted_iota(jnp.int32, (sz, tt), 0)
        pieces.append(jnp.where(b_iota < _L2_VALID[a], sv1[a:a + 1, :] + sv2[0:sz, :], -jnp.inf))
    cand0 = jnp.concatenate(pieces, axis=0)
    best0 = sv1[0:1, :] + sv2[0:1, :]
    cand = cand0
    z = jnp.zeros((1, tt), F32)
    for _ in range(P_TOPK):
        m, hit = _knock_out_max(cand, exact)
        cand = jnp.where(hit, -jnp.inf, cand)
        z = z + jnp.exp(m - best0)
    sel = jnp.where(cand != cand0, 1.0, 0.0)
    cnt1 = jnp.zeros((N_KEYS, tt), F32)
    off = 0
    for a in range(P_TOPK):
        c_a = jnp.sum(sel[off:off + _L2_SIZES[a], :], axis=0, keepdims=True)
        cnt1 = jnp.where(rank1 == float(a), c_a, cnt1)
        off += _L2_SIZES[a]
    count = lambda r: jnp.sum(jnp.where(r < float(P_TOPK), 1.0, 0.0), axis=0, keepdims=True)
    excess = (count(rank1) + count(rank2) + jnp.sum(sel, axis=0, keepdims=True)) - 3.0 * P_TOPK
    return jnp.exp(s1 - sv1[0:1, :]), cnt1, jnp.exp(s2 - sv2[0:1, :]) / z, rank2, excess


def _route_kernel(qt_ref, sk_ref, e1_ref, c1_ref, e2_ref, r2_ref):
    s1 = jnp.dot(sk_ref[0, 0], qt_ref[0:N_KEYS, :], preferred_element_type=F32)
    s2 = jnp.dot(sk_ref[0, 1], qt_ref[N_KEYS:2 * N_KEYS, :], preferred_element_type=F32)

    def emit(exact):
        e1, cnt1, e2, rank2, excess = _route(s1, s2, exact)
        e1_ref[0] = e1
        c1_ref[0] = cnt1
        e2_ref[0] = e2.astype(e2_ref.dtype)
        r2_ref[0] = rank2.astype(r2_ref.dtype)
        return excess

    excess = emit(False)

    @pl.when(jnp.max(excess) > 0.0)
    def _():
        emit(True)


def _peer_route(qt, sub_keys, tt):
    n = qt.shape[1]
    dk = sub_keys.shape[-1]
    out = lambda dt: jax.ShapeDtypeStruct((P_HEADS, N_KEYS, n), dt)
    ospec = pl.BlockSpec((1, N_KEYS, tt), lambda i, h: (h, 0, i))
    return pl.pallas_call(
        _route_kernel, grid=(n // tt, P_HEADS),
        in_specs=[pl.BlockSpec((2 * dk, tt), lambda i, h: (h, i)),
                  pl.BlockSpec((1, 2, N_KEYS, dk), lambda i, h: (h, 0, 0, 0))],
        out_specs=[ospec] * 4, out_shape=[out(F32), out(F32), out(CDT), out(CDT)],
        compiler_params=_params(("parallel", "parallel")), name="peer_route")(qt, sub_keys)


def _gelu(x):
    return 0.5 * x * (1.0 + lax.erf(x * (2.0 ** -0.5)))


def _peer_dense_kernel(ht_ref, eu_ref, evt_ref, e1_ref, c1_ref, e2_ref, r2_ref, o_ref, acc_ref, gate_ref, *, rows):
    k = pl.program_id(1)

    @pl.when(k == 0)
    def _():
        acc_ref[...] = jnp.zeros_like(acc_ref)

    gdt = gate_ref.dtype
    tt = ht_ref.shape[1]
    zero = jnp.zeros((), gdt)

    def sublane_bcast(row_ref, h, i):
        tile = jnp.broadcast_to(row_ref[h, pl.ds(i, 1), :], (16, tt)).astype(gdt)
        return jnp.concatenate([tile] * (N_KEYS // 16), axis=0)

    for ii in range(rows):
        i = k * rows + ii
        g = None
        for h in range(P_HEADS):
            term = jnp.where(r2_ref[h] < sublane_bcast(c1_ref, h, i), e2_ref[h], zero) * sublane_bcast(e1_ref, h, i)
            g = term if g is None else g + term
        gate_ref[ii * N_KEYS:(ii + 1) * N_KEYS, :] = g

    half = rows * N_KEYS // 2
    ht = ht_ref[...]
    pre = [jnp.dot(eu_ref[j * half:(j + 1) * half, :], ht, preferred_element_type=F32) for j in range(2)]
    for j in range(2):
        cols = slice(j * half, (j + 1) * half)
        w = (gate_ref[cols, :] * _gelu(pre[j]).astype(gdt)).astype(evt_ref.dtype)
        acc_ref[...] += jnp.dot(evt_ref[:, cols], w, preferred_element_type=F32)

    @pl.when(k == pl.num_programs(1) - 1)
    def _():
        o_ref[...] = acc_ref[...].T


def _peer_dense(ht, eu, evt, e1, c1, e2, r2, tt, rows):
    d, n = ht.shape
    n_exp = eu.shape[0]
    eb = rows * N_KEYS
    rt = pl.BlockSpec((P_HEADS, N_KEYS, tt), lambda i, k: (0, 0, i))
    return pl.pallas_call(
        functools.partial(_peer_dense_kernel, rows=rows),
        grid=(n // tt, n_exp // eb),
        in_specs=[pl.BlockSpec((d, tt), lambda i, k: (0, i)),
                  pl.BlockSpec((eb, d), lambda i, k: (k, 0)),
                  pl.BlockSpec((d, eb), lambda i, k: (0, k)),
                  rt, rt, rt, rt],
        out_specs=pl.BlockSpec((tt, d), lambda i, k: (i, 0)),
        out_shape=jax.ShapeDtypeStruct((n, d), F32),
        scratch_shapes=[pltpu.VMEM((d, tt), F32), pltpu.VMEM((eb, tt), e2.dtype)],
        compiler_params=_params(("parallel", "arbitrary")), name="peer_dense")(ht, eu, evt, e1, c1, e2, r2)


def _peer(ht, wqt, sub_keys, eu, evt):
    n = ht.shape[1]
    tt = 512 if n % 512 == 0 else 256
    qt = _matmul(wqt, ht, CDT, 512, tt)
    e1, c1, e2, r2 = _peer_route(qt, sub_keys, 256)
    return _peer_dense(ht, eu, evt, e1, c1, e2, r2, tt, rows=8)


_SRC = {}
_off = 0
for _name, _w in (("a_q", 512), ("a_k", 128), ("a_v", 128), ("b_q", 512), ("b_k", 128), ("b_v", 128),
                  ("c_q", 512), ("c_k", 512), ("c_v", 512), ("d_z", 1024), ("d_xs", 1024), ("d_b", 256),
                  ("d_c", 256), ("d_dt", 32), ("g_a", 1024), ("g_b", 1024), ("g_c", 1024), ("g_d", 1024)):
    _SRC[_name] = (_off, _w)
    _off += _w
_ORDER = ("d_z", "d_xs", "g_a", "g_b", "g_c", "g_d", "a_q", "b_q", "c_q", "c_k", "c_v",
          "d_b", "d_c", "a_k", "a_v", "b_k", "b_v")
_DST = {}
_off = 0
for _name in _ORDER:
    _DST[_name] = _off
    _off += _SRC[_name][1]
P_WIDTH = _off
DT_PAD = 128


def _blk(name, width):
    return _DST[name] // width


def _rope(x, cos, sin):
    w = x.shape[1]
    lane = lax.broadcasted_iota(jnp.int32, x.shape, 1)
    fwd = pltpu.roll(x, 16, 1)
    bwd = pltpu.roll(x, w - 16, 1)
    rot = jnp.where(lane % 32 < 16, -bwd, fwd)
    return x * cos + rot * sin


def _head_rms(x, bd):
    sq = x * x
    hi = sq.astype(jnp.bfloat16)
    lo = (sq - hi.astype(F32)).astype(jnp.bfloat16)
    ms = jnp.dot(hi, bd, preferred_element_type=F32) + jnp.dot(lo, bd, preferred_element_type=F32)
    return lax.rsqrt(ms + EPS)


def _qk_prep_kernel(q_ref, kv_ref, cs_ref, sn_ref, qg_ref, kg_ref, bd_ref, qo_ref, kvo_ref):
    cos2, sin2 = cs_ref[...], sn_ref[...]
    cos4 = jnp.concatenate([cos2] * 4, axis=1)
    sin4 = jnp.concatenate([sin2] * 4, axis=1)
    scale = HEAD_DIM ** -0.5
    bd = bd_ref[...]
    a_q = q_ref[:, 0:512].astype(F32)
    qo_ref[:, 0:512] = (_rope(a_q, cos4, sin4) * scale).astype(qo_ref.dtype)
    b_q = q_ref[:, 512:1024].astype(F32)
    b_q = b_q * _head_rms(b_q, bd) * qg_ref[...]
    qo_ref[:, 512:1024] = (_rope(b_q, cos4, sin4) * scale).astype(qo_ref.dtype)
    c_q = q_ref[:, 1024:1536].astype(F32)
    qo_ref[:, 1024:1536] = (_rope(c_q, cos4, sin4) * scale).astype(qo_ref.dtype)
    c_k = q_ref[:, 1536:2048].astype(F32)
    qo_ref[:, 1536:2048] = _rope(c_k, cos4, sin4).astype(qo_ref.dtype)
    a_k = kv_ref[:, 0:128].astype(F32)
    kvo_ref[:, 0:128] = _rope(a_k, cos2, sin2).astype(kvo_ref.dtype)
    kvo_ref[:, 128:256] = kv_ref[:, 128:256]
    b_k = kv_ref[:, 256:384].astype(F32)
    b_k = b_k * _head_rms(b_k, bd[0:128, 0:128]) * kg_ref[...]
    kvo_ref[:, 256:384] = _rope(b_k, cos2, sin2).astype(kvo_ref.dtype)
    kvo_ref[:, 384:512] = kv_ref[:, 384:512]


def _qk_prep(p, cos, sin, qn_g, kn_g, tiles_per_batch):
    n = p.shape[0]
    tt = TOK_TILE
    lane = jnp.arange(512)
    bd = jnp.where(lane[:, None] // HEAD_DIM == lane[None, :] // HEAD_DIM, 1.0 / HEAD_DIM, 0.0).astype(jnp.bfloat16)
    pos = pl.BlockSpec((tt, 128), lambda i: (i % tiles_per_batch, 0))
    full = lambda r, c: pl.BlockSpec((r, c), lambda i: (0, 0))
    return pl.pallas_call(
        _qk_prep_kernel, grid=(n // tt,),
        in_specs=[pl.BlockSpec((tt, 2048), lambda i: (i, _blk("a_q", 2048))),
                  pl.BlockSpec((tt, 512), lambda i: (i, _blk("a_k", 512))),
                  pos, pos, full(1, 512), full(1, 128), full(512, 512)],
        out_specs=[pl.BlockSpec((tt, 2048), lambda i: (i, 0)), pl.BlockSpec((tt, 512), lambda i: (i, 0))],
        out_shape=[jax.ShapeDtypeStruct((n, 2048), CDT), jax.ShapeDtypeStruct((n, 512), CDT)],
        compiler_params=_params(("parallel",)), name="qk_prep")(
            p, p, cos, sin, jnp.tile(qn_g, 8).reshape(1, 512), jnp.tile(kn_g, 2).reshape(1, 128), bd)


_NT = (((1,), (1,)), ((), ()))


def _stack_heads(ref, first, count):
    return jnp.concatenate([ref[:, (first + i) * HEAD_DIM:(first + i + 1) * HEAD_DIM] for i in range(count)], axis=0)


def _attn_a_kernel(q_ref, k_ref, v_ref, sink_ref, o_ref, *, n_ctx):
    tq = q_ref.shape[0]
    total = k_ref.shape[0]
    t = pl.program_id(1)
    n = t - n_ctx // tq
    kstart = pl.multiple_of(jnp.clip(n_ctx + (n - 1) * tq, n_ctx - tq, total - 3 * tq), tq)
    col = lax.broadcasted_iota(jnp.int32, (tq, n_ctx + 3 * tq), 1)
    qpos = n * tq + lax.broadcasted_iota(jnp.int32, (tq, n_ctx + 3 * tq), 0)
    kpos = kstart - 2 * n_ctx + col
    valid = (col < n_ctx) | ((jnp.abs(qpos - kpos) <= WINDOW) & (kpos >= 0) & (n >= 0))
    groups = A_HEADS // A_KV
    for g in range(A_KV):
        q = _stack_heads(q_ref, g * groups, groups)
        hs = slice(g * HEAD_DIM, (g + 1) * HEAD_DIM)
        k = jnp.concatenate([k_ref[0:n_ctx, hs], k_ref[pl.ds(kstart, 3 * tq), hs]], axis=0)
        v = jnp.concatenate([v_ref[0:n_ctx, hs], v_ref[pl.ds(kstart, 3 * tq), hs]], axis=0)
        s = lax.dot_general(q, k, _NT, preferred_element_type=F32)
        for i in range(groups):
            hq = g * groups + i
            sink = sink_ref[hq:hq + 1, 0:1]
            si = jnp.where(valid, s[i * tq:(i + 1) * tq], NEG)
            m = jnp.maximum(jnp.max(si, axis=1, keepdims=True), sink)
            e = jnp.exp(si - m)
            den = jnp.sum(e, axis=1, keepdims=True) + jnp.exp(sink - m)
            o = jnp.dot(e.astype(v.dtype), v, preferred_element_type=F32)
            o_ref[:, hq * HEAD_DIM:(hq + 1) * HEAD_DIM] = (o / den).astype(o_ref.dtype)


def _attn_a(q2, kv2, sink, batch, n_ctx):
    n = q2.shape[0]
    total = n // batch
    tq = WINDOW
    tpb = total // tq
    kvspec = lambda c: pl.BlockSpec((total, 128), lambda b, t: (b, c))
    return pl.pallas_call(
        functools.partial(_attn_a_kernel, n_ctx=n_ctx), grid=(batch, tpb),
        in_specs=[pl.BlockSpec((tq, 512), lambda b, t: (b * tpb + t, 0)), kvspec(0), kvspec(1),
                  pl.BlockSpec((A_HEADS, 128), lambda b, t: (0, 0))],
        out_specs=pl.BlockSpec((tq, 512), lambda b, t: (b * tpb + t, 0)),
        out_shape=jax.ShapeDtypeStruct((n, 512), CDT),
        compiler_params=_params(("parallel", "parallel")), name="attn_a")(
            q2, kv2, kv2, jnp.broadcast_to(sink.astype(F32)[:, None], (A_HEADS, 128)))


def _softmax_attend(q, k, v):
    s = lax.dot_general(q, k, _NT, preferred_element_type=F32)
    p = jnp.exp(s - jnp.max(s, axis=1, keepdims=True))
    l = jnp.sum(p, axis=1, keepdims=True)
    return jnp.dot(p.astype(v.dtype), v, preferred_element_type=F32) / l


def _on_key_range(k_ref, n_ctx, run):
    @pl.when(pl.program_id(1) == 0)
    def _():
        run(n_ctx)

    @pl.when(pl.program_id(1) != 0)
    def _():
        run(k_ref.shape[0])


def _attn_b_kernel(q_ref, k_ref, v_ref, o_ref, *, n_ctx):
    tq = q_ref.shape[0]
    groups = B_HEADS // B_KV
    stack = 2

    def run(nkeys):
        for g in range(B_KV):
            hs = slice(g * HEAD_DIM, (g + 1) * HEAD_DIM)
            k, v = k_ref[0:nkeys, hs], v_ref[0:nkeys, hs]
            for j in range(groups // stack):
                first = g * groups + j * stack
                o = _softmax_attend(_stack_heads(q_ref, first, stack), k, v)
                for i in range(stack):
                    hq = first + i
                    o_ref[:, hq * HEAD_DIM:(hq + 1) * HEAD_DIM] = o[i * tq:(i + 1) * tq].astype(o_ref.dtype)

    _on_key_range(k_ref, n_ctx, run)


def _attn_b(q2, kv2, batch, n_ctx):
    n = q2.shape[0]
    total = n // batch
    tq = TOK_TILE
    tpb = total // tq
    kvspec = lambda c: pl.BlockSpec((total, 128), lambda b, t: (b, c))
    return pl.pallas_call(
        functools.partial(_attn_b_kernel, n_ctx=n_ctx), grid=(batch, tpb),
        in_specs=[pl.BlockSpec((tq, 512), lambda b, t: (b * tpb + t, 1)), kvspec(2), kvspec(3)],
        out_specs=pl.BlockSpec((tq, 512), lambda b, t: (b * tpb + t, 0)),
        out_shape=jax.ShapeDtypeStruct((n, 512), CDT),
        compiler_params=_params(("parallel", "parallel")), name="attn_b")(q2, kv2, kv2)


def _attn_c_kernel(q_ref, k_ref, v_ref, lam_ref, g_ref, o_ref, *, n_ctx, out_scale):
    lam = lam_ref[...]

    def run(nkeys):
        for h in range(C_HEADS):
            c0 = 2 * h * HEAD_DIM
            v = v_ref[0:nkeys, c0:c0 + 2 * HEAD_DIM]
            maps = []
            for t in range(2):
                cols = slice(c0 + t * HEAD_DIM, c0 + (t + 1) * HEAD_DIM)
                maps.append(_softmax_attend(q_ref[:, cols], k_ref[0:nkeys, cols], v))
            o = maps[0] - lam * maps[1]
            o = o * lax.rsqrt(jnp.mean(o * o, axis=-1, keepdims=True) + EPS) * g_ref[...] * out_scale
            o_ref[:, c0:c0 + 2 * HEAD_DIM] = o.astype(o_ref.dtype)

    _on_key_range(k_ref, n_ctx, run)


def _attn_c(q2, p, lam, subln_g, lam_init, batch, n_ctx):
    n = q2.shape[0]
    total = n // batch
    tq = TOK_TILE
    tpb = total // tq
    vec = pl.BlockSpec((1, 128), lambda b, t: (0, 0))
    return pl.pallas_call(
        functools.partial(_attn_c_kernel, n_ctx=n_ctx, out_scale=1.0 - lam_init), grid=(batch, tpb),
        in_specs=[pl.BlockSpec((tq, 512), lambda b, t: (b * tpb + t, 2)),
                  pl.BlockSpec((total, 512), lambda b, t: (b, 3)),
                  pl.BlockSpec((total, 512), lambda b, t: (b, _blk("c_v", 512))), vec, vec],
        out_specs=pl.BlockSpec((tq, 512), lambda b, t: (b * tpb + t, 0)),
        out_shape=jax.ShapeDtypeStruct((n, 512), CDT),
        compiler_params=_params(("parallel", "parallel")), name="attn_c")(
            q2, q2, p, jnp.broadcast_to(lam.astype(F32), (1, 128)), subln_g.astype(F32).reshape(1, 128))


def _silu(x):
    return x / (1.0 + jnp.exp(-x))


def _conv3(cur, prev_row, next_row, w_ref, b_ref):
    t = cur.shape[0]
    row = lax.broadcasted_iota(jnp.int32, cur.shape, 0)
    up = jnp.where(row == 0, prev_row, pltpu.roll(cur, 1, 0))
    dn = jnp.where(row == t - 1, next_row, pltpu.roll(cur, t - 1, 0))
    return up * w_ref[0:1, :] + cur * w_ref[1:2, :] + dn * w_ref[2:3, :] + b_ref[...]


def _ssd_prep_kernel(xs_ref, xsp_ref, xsn_ref, bc_ref, bcp_ref, bcn_ref, dt_ref, wx_ref, bx_ref, wbc_ref, bbc_ref,
                     dtb_ref, a_ref, xo_ref, bco_ref, dto_ref, ao_ref, *, tiles_per_batch):
    tb = pl.program_id(0) % tiles_per_batch
    has_prev = (tb >= 2).astype(F32)
    has_next = ((tb != 0) & (tb != tiles_per_batch - 1)).astype(F32)
    hl = xsp_ref.shape[0]
    xs = _conv3(xs_ref[...].astype(F32), xsp_ref[hl - 1:hl, :].astype(F32) * has_prev,
                xsn_ref[0:1, :].astype(F32) * has_next, wx_ref, bx_ref)
    xo_ref[...] = _silu(xs)
    bc = _conv3(bc_ref[...].astype(F32), bcp_ref[hl - 1:hl, :].astype(F32) * has_prev,
                bcn_ref[0:1, :].astype(F32) * has_next, wbc_ref, bbc_ref)
    bco_ref[...] = _silu(bc).astype(bco_ref.dtype)
    u = dt_ref[...] + dtb_ref[...]
    dt = jnp.maximum(u, 0.0) + jnp.log1p(jnp.exp(-jnp.abs(u)))
    dto_ref[...] = dt
    ao_ref[...] = dt * a_ref[...]


def _ssd_prep(p, dt_raw, conv_w, conv_b, dt_bias, a_log, tiles_per_batch):
    n = p.shape[0]
    tt = TOK_TILE
    halo = 16
    per = tt // halo
    last = n // halo - 1
    cur = lambda name, w: pl.BlockSpec((tt, w), lambda i: (i, _blk(name, w)))
    prv = lambda name, w: pl.BlockSpec((halo, w), lambda i: (jnp.maximum(i * per - 1, 0), _blk(name, w)))
    nxt = lambda name, w: pl.BlockSpec((halo, w), lambda i: (jnp.minimum((i + 1) * per, last), _blk(name, w)))
    full = lambda r, c: pl.BlockSpec((r, c), lambda i: (0, 0))
    pad = lambda v: jnp.pad(v.astype(F32).reshape(1, -1), ((0, 0), (0, DT_PAD - 2 * M_HEADS)))
    cw, cb = conv_w.astype(F32), conv_b.astype(F32).reshape(1, -1)
    row = lambda w: pl.BlockSpec((tt, w), lambda i: (i, 0))
    return pl.pallas_call(
        functools.partial(_ssd_prep_kernel, tiles_per_batch=tiles_per_batch), grid=(n // tt,),
        in_specs=[cur("d_xs", 1024), prv("d_xs", 1024), nxt("d_xs", 1024),
                  cur("d_b", 512), prv("d_b", 512), nxt("d_b", 512), row(DT_PAD),
                  full(3, 1024), full(1, 1024), full(3, 512), full(1, 512), full(1, DT_PAD), full(1, DT_PAD)],
        out_specs=[row(1024), row(512), row(DT_PAD), row(DT_PAD)],
        out_shape=[jax.ShapeDtypeStruct((n, 1024), F32), jax.ShapeDtypeStruct((n, 512), CDT),
                   jax.ShapeDtypeStruct((n, DT_PAD), F32), jax.ShapeDtypeStruct((n, DT_PAD), F32)],
        compiler_params=_params(("parallel",)), name="ssd_prep")(
            p, p, p, p, p, p, dt_raw, cw[:, :M_INNER], cb[:, :M_INNER], cw[:, M_INNER:], cb[:, M_INNER:],
            pad(dt_bias), pad(-jnp.exp(a_log.astype(F32))))


def _split3(x):
    hi = x.astype(jnp.bfloat16)
    r = x - hi.astype(F32)
    mid = r.astype(jnp.bfloat16)
    lo = (r - mid.astype(F32)).astype(jnp.bfloat16)
    return hi, mid, lo


def _expand_heads(v, first, count, rows):
    lane = lax.broadcasted_iota(jnp.int32, (rows, 2 * M_HEAD_DIM), 1)
    return jnp.concatenate([jnp.where(lane < M_HEAD_DIM, v[:, c:c + 1], v[:, c + 1:c + 2])
                            for c in range(first, first + count, 2)], axis=1)


def _ssd_kernel(*refs, rev):
    if rev:
        xs_ref, bc_ref, dt_ref, a_ref, yf_ref, z_ref, dsk_ref, ng_ref, o_ref, st_ref = refs
    else:
        xs_ref, bc_ref, dt_ref, a_ref, o_ref, st_ref = refs
    t = xs_ref.shape[0]

    @pl.when(pl.program_id(1) == 0)
    def _():
        st_ref[...] = jnp.zeros_like(st_ref)

    li = lax.broadcasted_iota(jnp.int32, (t, t), 0)
    si = lax.broadcasted_iota(jnp.int32, (t, t), 1)
    tri = (si >= li) if rev else (li >= si)
    tri_b = jnp.where(tri, 1.0, 0.0).astype(jnp.bfloat16)
    cs = sum(jnp.dot(tri_b, part, preferred_element_type=F32) for part in _split3(a_ref[...]))
    cst = cs.T
    tot = cs[0:1, :] if rev else cs[t - 1:t, :]
    outdec, indec, sdec = jnp.exp(cs), jnp.exp(tot - cs), jnp.exp(tot)
    dt = dt_ref[...]
    dtt = dt.T
    d0 = M_HEADS if rev else 0
    first_head = lax.broadcasted_iota(jnp.int32, (t, 2 * M_HEAD_DIM), 1) < M_HEAD_DIM
    ys = []
    for g in range(M_GROUPS):
        c0 = d0 + g * M_HPG
        lanes = slice(g * M_HPG * M_HEAD_DIM, (g + 1) * M_HPG * M_HEAD_DIM)
        bg = bc_ref[:, g * M_STATE:(g + 1) * M_STATE]
        cg = bc_ref[:, (M_GROUPS + g) * M_STATE:(M_GROUPS + g + 1) * M_STATE]
        cb = lax.dot_general(cg, bg, _NT, preferred_element_type=F32)
        bgt = bg.astype(F32).T.astype(bg.dtype)
        xs = xs_ref[:, lanes]
        xsb = xs.astype(bg.dtype)
        hst = st_ref[g]
        y_off = jnp.dot(cg, hst.astype(cg.dtype), preferred_element_type=F32) * _expand_heads(outdec, c0, M_HPG, t)
        y_diag = []
        for k in range(M_HPG // 2):
            xpair = xsb[:, 2 * k * M_HEAD_DIM:2 * (k + 1) * M_HEAD_DIM]
            halves = []
            for c in (c0 + 2 * k, c0 + 2 * k + 1):
                lm = jnp.exp(jnp.where(tri, cs[:, c:c + 1] - cst[c:c + 1, :], NEG)) * dtt[c:c + 1, :]
                halves.append(jnp.dot((cb * lm).astype(xsb.dtype), xpair, preferred_element_type=F32))
            y_diag.append(jnp.where(first_head, halves[0], halves[1]))
        ys.append(y_off + jnp.concatenate(y_diag, axis=1))
        xin = (xs * _expand_heads(dt * indec, c0, M_HPG, t)).astype(bg.dtype)
        st_ref[g] = hst * _expand_heads(sdec, c0, M_HPG, 1) + jnp.dot(bgt, xin, preferred_element_type=F32)
    y = jnp.concatenate(ys, axis=1)
    if not rev:
        o_ref[...] = y
        return
    y = (y + yf_ref[...] + xs_ref[...] * dsk_ref[...]) * _silu(z_ref[...].astype(F32))
    half = M_INNER // M_GROUPS
    parts = []
    for g in range(M_GROUPS):
        yg = y[:, g * half:(g + 1) * half]
        parts.append(yg * lax.rsqrt(jnp.mean(yg * yg, axis=-1, keepdims=True) + EPS))
    o_ref[...] = (jnp.concatenate(parts, axis=1) * ng_ref[...]).astype(o_ref.dtype)


def _ssd(xs, bc, dt, a, batch, n_ctx, rev, yf=None, p=None, d_skip=None, norm_g=None):
    n = xs.shape[0]
    t = CHUNK
    cpb = n // batch // t
    cc = n_ctx // t

    def chunk(b, c):
        if rev:
            c = jnp.where(c < cc, cc - 1 - c, cpb - 1 - (c - cc))
        return b * cpb + c

    row = lambda w: pl.BlockSpec((t, w), lambda b, c: (chunk(b, c), 0))
    in_specs = [row(1024), row(512), row(DT_PAD), row(DT_PAD)]
    args = [xs, bc, dt, a]
    if rev:
        vec = pl.BlockSpec((1, 1024), lambda b, c: (0, 0))
        in_specs += [row(1024), pl.BlockSpec((t, 1024), lambda b, c: (chunk(b, c), _blk("d_z", 1024))), vec, vec]
        args += [yf, p, jnp.repeat(d_skip.astype(F32), M_HEAD_DIM).reshape(1, -1), norm_g.astype(F32).reshape(1, -1)]
    return pl.pallas_call(
        functools.partial(_ssd_kernel, rev=rev), grid=(batch, cpb), in_specs=in_specs,
        out_specs=row(1024), out_shape=jax.ShapeDtypeStruct((n, 1024), CDT if rev else F32),
        scratch_shapes=[pltpu.VMEM((M_GROUPS, M_STATE, M_HPG * M_HEAD_DIM), F32)],
        compiler_params=_params(("parallel", "arbitrary")), name="ssd_rev" if rev else "ssd_fwd")(*args)


def _merge_kernel(oa_ref, ob_ref, oc_ref, od_ref, ga_ref, gb_ref, gc_ref, gd_ref,
                  wa_ref, wb_ref, wc_ref, wd_ref, wo_ref, o_ref):
    acc = None
    for o, g, w in ((oa_ref, ga_ref, wa_ref), (ob_ref, gb_ref, wb_ref), (oc_ref, gc_ref, wc_ref), (od_ref, gd_ref, wd_ref)):
        gate = 1.0 / (1.0 + jnp.exp(-g[...].astype(F32)))
        term = gate * jnp.dot(o[...], w[...], preferred_element_type=F32)
        acc = term if acc is None else acc + term
    o_ref[...] = jnp.dot(acc.astype(wo_ref.dtype), wo_ref[...], preferred_element_type=F32)


def _merge(oa, ob, oc, od, p, w_br, w_out):
    n = oa.shape[0]
    d = w_out.shape[0]
    tt = TOK_TILE
    row = lambda w: pl.BlockSpec((tt, w), lambda i: (i, 0))
    gate = lambda name: pl.BlockSpec((tt, d), lambda i: (i, _blk(name, d)))
    full = lambda r: pl.BlockSpec((r, d), lambda i: (0, 0))
    return pl.pallas_call(
        _merge_kernel, grid=(n // tt,),
        in_specs=[row(512), row(512), row(512), row(1024), gate("g_a"), gate("g_b"), gate("g_c"), gate("g_d"),
                  full(512), full(512), full(512), full(1024), full(d)],
        out_specs=row(d), out_shape=jax.ShapeDtypeStruct((n, d), F32),
        compiler_params=_params(("parallel",)), name="merge")(
            oa, ob, oc, od, p, p, p, p, w_br[0:512], w_br[512:1024], w_br[1024:1536], w_br[1536:2560], w_out)


def _silu_rows_kernel(c_ref, o_ref):
    o_ref[...] = _silu(c_ref[...]).astype(o_ref.dtype)


def _rope_tables(rows, n_ctx):
    row = jnp.repeat(jnp.arange(rows, dtype=F32), GRID_W)
    col = jnp.tile(jnp.arange(GRID_W, dtype=F32), rows)
    nq = HEAD_DIM // 4
    inv = ROPE_THETA ** (-jnp.arange(nq, dtype=F32) / nq)
    ar, ac = row[:, None] * inv, col[:, None] * inv
    ang = jnp.concatenate([ar, ar, ac, ac], axis=-1)
    ang = jnp.concatenate([jnp.zeros((n_ctx, HEAD_DIM), F32), ang], axis=0)
    return jnp.tile(jnp.cos(ang), (1, 2)), jnp.tile(jnp.sin(ang), (1, 2))


def kernel(x, c, ctx, c_ctx, w_ada, b_ada, g_norm1, g_norm2, w_in, a_sink, b_qnorm, b_knorm, c_lam_q1, c_lam_k1, c_lam_q2, c_lam_k2, c_subln, m_conv_w, m_conv_b, m_dt_bias, m_a_log, m_d, m_norm, w_br, w_out, p_wq, p_subkeys, p_u, p_v, g_final):
    batch, s, d = x.shape
    n_ctx = ctx.shape[1]
    depth = w_in.shape[0]
    assert n_ctx == TOK_TILE and s % TOK_TILE == 0 and s % GRID_W == 0
    tpb = (s + n_ctx) // TOK_TILE
    xc = jnp.concatenate([ctx, x], axis=1).reshape(-1, d)
    cos, sin = _rope_tables(s // GRID_W, n_ctx)

    cond = jnp.concatenate([c, c_ctx[None, :], jnp.zeros((16 - batch - 1, d), F32)], axis=0)
    cond = pl.pallas_call(_silu_rows_kernel, out_shape=jax.ShapeDtypeStruct(cond.shape, CDT), name="silu")(cond)

    def mod_rows(m):
        m6 = m.reshape(16, 6, d)
        lat, con = m6[:batch], jnp.broadcast_to(m6[batch][None], (batch, 6, d))
        both = jnp.stack([con, lat], axis=1).reshape(2 * batch, 6, d)
        return [both[:, k][:, None, :] for k in range(6)]

    w_proj = jnp.concatenate([w_in[:, :, _SRC[nm][0]:_SRC[nm][0] + _SRC[nm][1]] for nm in _ORDER], axis=-1).astype(CDT)
    o_dt = _SRC["d_dt"][0]
    w_dt = jnp.pad(w_in[:, :, o_dt:o_dt + 2 * M_HEADS], ((0, 0), (0, 0), (0, DT_PAD - 2 * M_HEADS))).astype(CDT)

    delta, gate_prev = None, None
    for l in range(depth):
        mods = _matmul(cond, w_ada[l].astype(CDT), F32, 16, 768, bias=b_ada[l].reshape(1, -1))
        sh1, sc1, gt1, sh2, sc2, gt2 = mod_rows(mods)
        xc, h = _norm_mod(xc, delta, gate_prev, g_norm1[l], sh1, sc1, tpb)
        tm = next(t for t in (8 * TOK_TILE, 2 * TOK_TILE, TOK_TILE) if h.shape[0] % t == 0)
        p = _matmul(h, w_proj[l], CDT, tm, 512)
        dt_raw = _matmul(h, w_dt[l], F32, tm, DT_PAD)
        q2, kv2 = _qk_prep(p, cos, sin, b_qnorm[l].astype(F32), b_knorm[l].astype(F32), tpb)
        o_a = _attn_a(q2, kv2, a_sink[l], batch, n_ctx)
        o_b = _attn_b(q2, kv2, batch, n_ctx)
        f32 = lambda v: v.astype(F32)
        lam_init = 0.8 - 0.6 * math.exp(-0.3 * l)
        lam = (jnp.exp(jnp.sum(f32(c_lam_q1[l]) * f32(c_lam_k1[l])))
               - jnp.exp(jnp.sum(f32(c_lam_q2[l]) * f32(c_lam_k2[l]))) + lam_init)
        o_c = _attn_c(q2, p, lam, c_subln[l], lam_init, batch, n_ctx)
        xs, bc, dt, a = _ssd_prep(p, dt_raw, m_conv_w[l], m_conv_b[l], m_dt_bias[l].reshape(-1), m_a_log[l].reshape(-1), tpb)
        y_f = _ssd(xs, bc, dt, a, batch, n_ctx, rev=False)
        o_d = _ssd(xs, bc, dt, a, batch, n_ctx, rev=True, yf=y_f, p=p, d_skip=m_d[l], norm_g=m_norm[l])
        mix = _merge(o_a, o_b, o_c, o_d, p, w_br[l].astype(CDT), w_out[l].astype(CDT))
        xc, _, ht = _norm_mod(xc, mix, gt1, g_norm2[l], sh2, sc2, tpb, want_t=True)
        delta = _peer(ht, p_wq[l].T.astype(CDT), p_subkeys[l].astype(CDT), p_u[l].astype(CDT), p_v[l].T.astype(CDT))
        gate_prev = gt2
    zero = jnp.zeros((2 * batch, 1, d), F32)
    _, y = _norm_mod(xc, delta, gate_prev, g_final, zero, zero, tpb, out_dtype=F32)
    return y.reshape(batch, n_ctx + s, d)[:, n_ctx:]
```

```python
import functools
import math

import jax
import jax.numpy as jnp
from jax import lax
from jax.experimental import pallas as pl
from jax.experimental.pallas import tpu as pltpu

F32 = jnp.float32
CDT = jnp.bfloat16

EPS = 1e-6
NEG = -1e30
HEAD_DIM = 64
GRID_W = 64
ROPE_THETA = 10000.0
WINDOW = 128
A_HEADS, A_KV = 8, 2
B_HEADS, B_KV = 8, 2
C_HEADS = 4
M_HEADS, M_HEAD_DIM, M_GROUPS, M_STATE = 16, 64, 2, 128
M_INNER = M_HEADS * M_HEAD_DIM
M_HPG = M_HEADS // M_GROUPS
CHUNK = 128
P_HEADS, N_KEYS, P_TOPK = 8, 128, 16
TOK_TILE = 256

VMEM_LIMIT = 48 * 1024 * 1024


def _params(sem):
    return pltpu.CompilerParams(dimension_semantics=sem, vmem_limit_bytes=VMEM_LIMIT)


def _mm_kernel(a_ref, b_ref, o_ref):
    o_ref[...] = jnp.dot(a_ref[...], b_ref[...], preferred_element_type=F32).astype(o_ref.dtype)


def _mm_bias_kernel(a_ref, b_ref, bias_ref, o_ref):
    acc = jnp.dot(a_ref[...], b_ref[...], preferred_element_type=F32)
    o_ref[...] = (acc + bias_ref[...]).astype(o_ref.dtype)


def _matmul(a, b, out_dtype, tm, tn, bias=None):
    m, k = a.shape
    n = b.shape[1]
    assert m % tm == 0 and n % tn == 0, (a.shape, b.shape, tm, tn)
    in_specs = [pl.BlockSpec((tm, k), lambda i, j: (i, 0)),
                pl.BlockSpec((k, tn), lambda i, j: (0, j))]
    args = [a, b]
    kern = _mm_kernel
    if bias is not None:
        in_specs.append(pl.BlockSpec((1, tn), lambda i, j: (0, j)))
        args.append(bias)
        kern = _mm_bias_kernel
    return pl.pallas_call(
        kern, grid=(m // tm, n // tn), in_specs=in_specs,
        out_specs=pl.BlockSpec((tm, tn), lambda i, j: (i, j)),
        out_shape=jax.ShapeDtypeStruct((m, n), out_dtype),
        compiler_params=_params(("parallel", "parallel")), name="matmul")(*args)


def _norm_mod_kernel(*refs, has_delta, want_t):
    it = iter(refs)
    x_ref = next(it)
    if has_delta:
        d_ref, gp_ref = next(it), next(it)
    g_ref, sh_ref, sc_ref = next(it), next(it), next(it)
    if has_delta:
        xo_ref = next(it)
    h_ref = next(it)
    x = x_ref[...]
    if has_delta:
        x = x + gp_ref[0] * d_ref[...]
        xo_ref[...] = x
    y = x * lax.rsqrt(jnp.mean(x * x, axis=-1, keepdims=True) + EPS) * g_ref[...]
    h = y * (1.0 + sc_ref[0]) + sh_ref[0]
    h_ref[...] = h.astype(h_ref.dtype)
    if want_t:
        ht_ref = next(it)
        ht_ref[...] = h.T.astype(ht_ref.dtype)


def _norm_mod(x, delta, gate_prev, g, shift, scale, tiles_per_batch, want_t=False, out_dtype=None):
    n, d = x.shape
    tt = TOK_TILE
    has_delta = delta is not None
    out_dtype = out_dtype or CDT

    def mod_idx(i):
        return ((i // tiles_per_batch) * 2 + jnp.minimum(i % tiles_per_batch, 1), 0, 0)

    row = pl.BlockSpec((tt, d), lambda i: (i, 0))
    mod = pl.BlockSpec((1, 1, d), mod_idx)
    in_specs, args = [row], [x]
    if has_delta:
        in_specs += [row, mod]
        args += [delta, gate_prev]
    in_specs += [pl.BlockSpec((1, d), lambda i: (0, 0)), mod, mod]
    args += [g.reshape(1, d), shift, scale]
    out_specs, out_shape = [], []
    if has_delta:
        out_specs.append(row)
        out_shape.append(jax.ShapeDtypeStruct((n, d), F32))
    out_specs.append(row)
    out_shape.append(jax.ShapeDtypeStruct((n, d), out_dtype))
    if want_t:
        out_specs.append(pl.BlockSpec((d, tt), lambda i: (0, i)))
        out_shape.append(jax.ShapeDtypeStruct((d, n), CDT))
    outs = pl.pallas_call(
        functools.partial(_norm_mod_kernel, has_delta=has_delta, want_t=want_t),
        grid=(n // tt,), in_specs=in_specs, out_specs=out_specs, out_shape=out_shape,
        compiler_params=_params(("parallel",)), name="norm_mod")(*args)
    outs = list(outs)
    x_new = outs.pop(0) if has_delta else x
    return (x_new, *outs)


_L2_SIZES = tuple(16 if a == 0 else 8 for a in range(P_TOPK))
_L2_VALID = tuple(P_TOPK // (a + 1) for a in range(P_TOPK))
_L2_ROWS = sum(_L2_SIZES)


def _knock_out_max(work, exact):
    m = jnp.max(work, axis=0, keepdims=True)
    hit = work == m
    if exact:
        row = lax.broadcasted_iota(jnp.int32, work.shape, 0)
        hit = row == jnp.min(jnp.where(hit, row, work.shape[0]), axis=0, keepdims=True)
    return m, hit


def _top16(s, exact):
    rank = jnp.full(s.shape, 127.0, F32)
    vals = []
    work = s
    for a in range(P_TOPK):
        m, hit = _knock_out_max(work, exact)
        rank = jnp.where(hit, float(a), rank)
        work = jnp.where(hit, -jnp.inf, work)
        vals.append(m)
    return jnp.concatenate(vals, axis=0), rank


def _route(s1, s2, exact):
    tt = s1.shape[1]
    sv1, rank1 = _top16(s1, exact)
    sv2, rank2 = _top16(s2, exact)
    pieces = []
    for a in range(P_TOPK):
        sz = _L2_SIZES[a]
        b_iota = lax.broadcasted_iota(jnp.int32, (sz, tt), 0)
        pieces.append(jnp.where(b_iota < _L2_VALID[a], sv1[a:a + 1, :] + sv2[0:sz, :], -jnp.inf))
    cand0 = jnp.concatenate(pieces, axis=0)
    best0 = sv1[0:1, :] + sv2[0:1, :]
    cand = cand0
    z = jnp.zeros((1, tt), F32)
    for _ in range(P_TOPK):
        m, hit = _knock_out_max(cand, exact)
        cand = jnp.where(hit, -jnp.inf, cand)
        z = z + jnp.exp(m - best0)
    sel = jnp.where(cand != cand0, 1.0, 0.0)
    cnt1 = jnp.zeros((N_KEYS, tt), F32)
    off = 0
    for a in range(P_TOPK):
        c_a = jnp.sum(sel[off:off + _L2_SIZES[a], :], axis=0, keepdims=True)
        cnt1 = jnp.where(rank1 == float(a), c_a, cnt1)
        off += _L2_SIZES[a]
    count = lambda r: jnp.sum(jnp.where(r < float(P_TOPK), 1.0, 0.0), axis=0, keepdims=True)
    excess = (count(rank1) + count(rank2) + jnp.sum(sel, axis=0, keepdims=True)) - 3.0 * P_TOPK
    return jnp.exp(s1 - sv1[0:1, :]), cnt1, jnp.exp(s2 - sv2[0:1, :]) / z, rank2, excess


def _route_kernel(qt_ref, sk_ref, e1_ref, c1_ref, e2_ref, r2_ref):
    s1 = jnp.dot(sk_ref[0, 0], qt_ref[0:N_KEYS, :], preferred_element_type=F32)
    s2 = jnp.dot(sk_ref[0, 1], qt_ref[N_KEYS:2 * N_KEYS, :], preferred_element_type=F32)

    def emit(exact):
        e1, cnt1, e2, rank2, excess = _route(s1, s2, exact)
        e1_ref[0] = e1
        c1_ref[0] = cnt1
        e2_ref[0] = e2.astype(e2_ref.dtype)
        r2_ref[0] = rank2.astype(r2_ref.dtype)
        return excess

    excess = emit(False)

    @pl.when(jnp.max(excess) > 0.0)
    def _():
        emit(True)


def _peer_route(qt, sub_keys, tt):
    n = qt.shape[1]
    dk = sub_keys.shape[-1]
    out = lambda dt: jax.ShapeDtypeStruct((P_HEADS, N_KEYS, n), dt)
    ospec = pl.BlockSpec((1, N_KEYS, tt), lambda i, h: (h, 0, i))
    return pl.pallas_call(
        _route_kernel, grid=(n // tt, P_HEADS),
        in_specs=[pl.BlockSpec((2 * dk, tt), lambda i, h: (h, i)),
                  pl.BlockSpec((1, 2, N_KEYS, dk), lambda i, h: (h, 0, 0, 0))],
        out_specs=[ospec] * 4, out_shape=[out(F32), out(F32), out(CDT), out(CDT)],
        compiler_params=_params(("parallel", "parallel")), name="peer_route")(qt, sub_keys)


def _gelu(x):
    return 0.5 * x * (1.0 + lax.erf(x * (2.0 ** -0.5)))


def _peer_dense_kernel(ht_ref, eu_ref, evt_ref, e1_ref, c1_ref, e2_ref, r2_ref, o_ref, acc_ref, gate_ref, *, rows):
    k = pl.program_id(1)

    @pl.when(k == 0)
    def _():
        acc_ref[...] = jnp.zeros_like(acc_ref)

    gdt = gate_ref.dtype
    tt = ht_ref.shape[1]
    zero = jnp.zeros((), gdt)

    def sublane_bcast(row_ref, h, i):
        tile = jnp.broadcast_to(row_ref[h, pl.ds(i, 1), :], (16, tt)).astype(gdt)
        return jnp.concatenate([tile] * (N_KEYS // 16), axis=0)

    for ii in range(rows):
        i = k * rows + ii
        g = None
        for h in range(P_HEADS):
            term = jnp.where(r2_ref[h] < sublane_bcast(c1_ref, h, i), e2_ref[h], zero) * sublane_bcast(e1_ref, h, i)
            g = term if g is None else g + term
        gate_ref[ii * N_KEYS:(ii + 1) * N_KEYS, :] = g

    parts = 4
    sub = rows * N_KEYS // parts
    ht = ht_ref[...]
    pre = [jnp.dot(eu_ref[j * sub:(j + 1) * sub, :], ht, preferred_element_type=F32) for j in range(parts)]
    for j in range(parts):
        cols = slice(j * sub, (j + 1) * sub)
        w = (gate_ref[cols, :] * _gelu(pre[j]).astype(gdt)).astype(evt_ref.dtype)
        acc_ref[...] += jnp.dot(evt_ref[:, cols], w, preferred_element_type=F32)

    @pl.when(k == pl.num_programs(1) - 1)
    def _():
        o_ref[...] = acc_ref[...].T


def _peer_dense(ht, eu, evt, e1, c1, e2, r2, tt, rows):
    d, n = ht.shape
    n_exp = eu.shape[0]
    eb = rows * N_KEYS
    rt = pl.BlockSpec((P_HEADS, N_KEYS, tt), lambda i, k: (0, 0, i))
    return pl.pallas_call(
        functools.partial(_peer_dense_kernel, rows=rows),
        grid=(n // tt, n_exp // eb),
        in_specs=[pl.BlockSpec((d, tt), lambda i, k: (0, i)),
                  pl.BlockSpec((eb, d), lambda i, k: (k, 0)),
                  pl.BlockSpec((d, eb), lambda i, k: (0, k)),
                  rt, rt, rt, rt],
        out_specs=pl.BlockSpec((tt, d), lambda i, k: (i, 0)),
        out_shape=jax.ShapeDtypeStruct((n, d), F32),
        scratch_shapes=[pltpu.VMEM((d, tt), F32), pltpu.VMEM((eb, tt), e2.dtype)],
        compiler_params=_params(("parallel", "arbitrary")), name="peer_dense")(ht, eu, evt, e1, c1, e2, r2)


def _peer(ht, wqt, sub_keys, eu, evt):
    n = ht.shape[1]
    tt = 512 if n % 512 == 0 else 256
    qt = _matmul(wqt, ht, CDT, wqt.shape[0], tt)
    e1, c1, e2, r2 = _peer_route(qt, sub_keys, 256)
    return _peer_dense(ht, eu, evt, e1, c1, e2, r2, tt, rows=8)


_SRC = {}
_off = 0
for _name, _w in (("a_q", 512), ("a_k", 128), ("a_v", 128), ("b_q", 512), ("b_k", 128), ("b_v", 128),
                  ("c_q", 512), ("c_k", 512), ("c_v", 512), ("d_z", 1024), ("d_xs", 1024), ("d_b", 256),
                  ("d_c", 256), ("d_dt", 32), ("g_a", 1024), ("g_b", 1024), ("g_c", 1024), ("g_d", 1024)):
    _SRC[_name] = (_off, _w)
    _off += _w
_ORDER = ("d_z", "d_xs", "g_a", "g_b", "g_c", "g_d", "a_q", "b_q", "c_q", "c_k", "c_v",
          "d_b", "d_c", "a_k", "a_v", "b_k", "b_v")
_DST = {}
_off = 0
for _name in _ORDER:
    _DST[_name] = _off
    _off += _SRC[_name][1]
P_WIDTH = _off
DT_PAD = 128


def _blk(name, width):
    return _DST[name] // width


def _rope(x, cos, sin):
    w = x.shape[1]
    lane = lax.broadcasted_iota(jnp.int32, x.shape, 1)
    fwd = pltpu.roll(x, 16, 1)
    bwd = pltpu.roll(x, w - 16, 1)
    rot = jnp.where(lane % 32 < 16, -bwd, fwd)
    return x * cos + rot * sin


def _head_rms(x, bd):
    sq = x * x
    hi = sq.astype(jnp.bfloat16)
    lo = (sq - hi.astype(F32)).astype(jnp.bfloat16)
    ms = jnp.dot(hi, bd, preferred_element_type=F32) + jnp.dot(lo, bd, preferred_element_type=F32)
    return lax.rsqrt(ms + EPS)


def _qk_prep_kernel(q_ref, kv_ref, cs_ref, sn_ref, qg_ref, kg_ref, bd_ref, qo_ref, kvo_ref):
    cos2, sin2 = cs_ref[...], sn_ref[...]
    cos4 = jnp.concatenate([cos2] * 4, axis=1)
    sin4 = jnp.concatenate([sin2] * 4, axis=1)
    scale = HEAD_DIM ** -0.5
    bd = bd_ref[...]
    a_q = q_ref[:, 0:512].astype(F32)
    qo_ref[:, 0:512] = (_rope(a_q, cos4, sin4) * scale).astype(qo_ref.dtype)
    b_q = q_ref[:, 512:1024].astype(F32)
    b_q = b_q * _head_rms(b_q, bd) * qg_ref[...]
    qo_ref[:, 512:1024] = (_rope(b_q, cos4, sin4) * scale).astype(qo_ref.dtype)
    c_q = q_ref[:, 1024:1536].astype(F32)
    qo_ref[:, 1024:1536] = (_rope(c_q, cos4, sin4) * scale).astype(qo_ref.dtype)
    c_k = q_ref[:, 1536:2048].astype(F32)
    qo_ref[:, 1536:2048] = _rope(c_k, cos4, sin4).astype(qo_ref.dtype)
    a_k = kv_ref[:, 0:128].astype(F32)
    kvo_ref[:, 0:128] = _rope(a_k, cos2, sin2).astype(kvo_ref.dtype)
    kvo_ref[:, 128:256] = kv_ref[:, 128:256]
    b_k = kv_ref[:, 256:384].astype(F32)
    b_k = b_k * _head_rms(b_k, bd[0:128, 0:128]) * kg_ref[...]
    kvo_ref[:, 256:384] = _rope(b_k, cos2, sin2).astype(kvo_ref.dtype)
    kvo_ref[:, 384:512] = kv_ref[:, 384:512]


def _qk_prep(p, cos, sin, qn_g, kn_g, tiles_per_batch):
    n = p.shape[0]
    tt = TOK_TILE
    lane = jnp.arange(512)
    bd = jnp.where(lane[:, None] // HEAD_DIM == lane[None, :] // HEAD_DIM, 1.0 / HEAD_DIM, 0.0).astype(jnp.bfloat16)
    pos = pl.BlockSpec((tt, 128), lambda i: (i % tiles_per_batch, 0))
    full = lambda r, c: pl.BlockSpec((r, c), lambda i: (0, 0))
    return pl.pallas_call(
        _qk_prep_kernel, grid=(n // tt,),
        in_specs=[pl.BlockSpec((tt, 2048), lambda i: (i, _blk("a_q", 2048))),
                  pl.BlockSpec((tt, 512), lambda i: (i, _blk("a_k", 512))),
                  pos, pos, full(1, 512), full(1, 128), full(512, 512)],
        out_specs=[pl.BlockSpec((tt, 2048), lambda i: (i, 0)), pl.BlockSpec((tt, 512), lambda i: (i, 0))],
        out_shape=[jax.ShapeDtypeStruct((n, 2048), CDT), jax.ShapeDtypeStruct((n, 512), CDT)],
        compiler_params=_params(("parallel",)), name="qk_prep")(
            p, p, cos, sin, jnp.tile(qn_g, 8).reshape(1, 512), jnp.tile(kn_g, 2).reshape(1, 128), bd)


_NT = (((1,), (1,)), ((), ()))


def _stack_heads(ref, first, count):
    return jnp.concatenate([ref[:, (first + i) * HEAD_DIM:(first + i + 1) * HEAD_DIM] for i in range(count)], axis=0)


def _attn_a_kernel(q_ref, k_ref, v_ref, sink_ref, o_ref, *, n_ctx):
    tq = q_ref.shape[0]
    total = k_ref.shape[0]
    t = pl.program_id(1)
    n = t - n_ctx // tq
    kstart = pl.multiple_of(jnp.clip(n_ctx + (n - 1) * tq, n_ctx - tq, total - 3 * tq), tq)
    col = lax.broadcasted_iota(jnp.int32, (tq, n_ctx + 3 * tq), 1)
    qpos = n * tq + lax.broadcasted_iota(jnp.int32, (tq, n_ctx + 3 * tq), 0)
    kpos = kstart - 2 * n_ctx + col
    valid = (col < n_ctx) | ((jnp.abs(qpos - kpos) <= WINDOW) & (kpos >= 0) & (n >= 0))
    groups = A_HEADS // A_KV
    for g in range(A_KV):
        q = _stack_heads(q_ref, g * groups, groups)
        hs = slice(g * HEAD_DIM, (g + 1) * HEAD_DIM)
        k = jnp.concatenate([k_ref[0:n_ctx, hs], k_ref[pl.ds(kstart, 3 * tq), hs]], axis=0)
        v = jnp.concatenate([v_ref[0:n_ctx, hs], v_ref[pl.ds(kstart, 3 * tq), hs]], axis=0)
        s = lax.dot_general(q, k, _NT, preferred_element_type=F32)
        for i in range(groups):
            hq = g * groups + i
            sink = sink_ref[hq:hq + 1, 0:1]
            si = jnp.where(valid, s[i * tq:(i + 1) * tq], NEG)
            m = jnp.maximum(jnp.max(si, axis=1, keepdims=True), sink)
            e = jnp.exp(si - m)
            den = jnp.sum(e, axis=1, keepdims=True) + jnp.exp(sink - m)
            o = jnp.dot(e.astype(v.dtype), v, preferred_element_type=F32)
            o_ref[:, hq * HEAD_DIM:(hq + 1) * HEAD_DIM] = (o / den).astype(o_ref.dtype)


def _attn_a(q2, kv2, sink, batch, n_ctx):
    n = q2.shape[0]
    total = n // batch
    tq = WINDOW
    tpb = total // tq
    kvspec = lambda c: pl.BlockSpec((total, 128), lambda b, t: (b, c))
    return pl.pallas_call(
        functools.partial(_attn_a_kernel, n_ctx=n_ctx), grid=(batch, tpb),
        in_specs=[pl.BlockSpec((tq, 512), lambda b, t: (b * tpb + t, 0)), kvspec(0), kvspec(1),
                  pl.BlockSpec((A_HEADS, 128), lambda b, t: (0, 0))],
        out_specs=pl.BlockSpec((tq, 512), lambda b, t: (b * tpb + t, 0)),
        out_shape=jax.ShapeDtypeStruct((n, 512), CDT),
        compiler_params=_params(("parallel", "parallel")), name="attn_a")(
            q2, kv2, kv2, jnp.broadcast_to(sink.astype(F32)[:, None], (A_HEADS, 128)))


def _softmax_attend(q, k, v):
    s = lax.dot_general(q, k, _NT, preferred_element_type=F32)
    p = jnp.exp(s - jnp.max(s, axis=1, keepdims=True))
    l = jnp.sum(p, axis=1, keepdims=True)
    return jnp.dot(p.astype(v.dtype), v, preferred_element_type=F32) / l


def _on_key_range(k_ref, n_ctx, run):
    @pl.when(pl.program_id(1) == 0)
    def _():
        run(n_ctx)

    @pl.when(pl.program_id(1) != 0)
    def _():
        run(k_ref.shape[0])


def _attn_b_kernel(q_ref, k_ref, v_ref, o_ref, *, n_ctx):
    tq = q_ref.shape[0]
    groups = B_HEADS // B_KV
    stack = 1

    def run(nkeys):
        for g in range(B_KV):
            hs = slice(g * HEAD_DIM, (g + 1) * HEAD_DIM)
            k, v = k_ref[0:nkeys, hs], v_ref[0:nkeys, hs]
            for j in range(groups // stack):
                first = g * groups + j * stack
                o = _softmax_attend(_stack_heads(q_ref, first, stack), k, v)
                for i in range(stack):
                    hq = first + i
                    o_ref[:, hq * HEAD_DIM:(hq + 1) * HEAD_DIM] = o[i * tq:(i + 1) * tq].astype(o_ref.dtype)

    _on_key_range(k_ref, n_ctx, run)


def _attn_b(q2, kv2, batch, n_ctx):
    n = q2.shape[0]
    total = n // batch
    tq = TOK_TILE
    tpb = total // tq
    kvspec = lambda c: pl.BlockSpec((total, 128), lambda b, t: (b, c))
    return pl.pallas_call(
        functools.partial(_attn_b_kernel, n_ctx=n_ctx), grid=(batch, tpb),
        in_specs=[pl.BlockSpec((tq, 512), lambda b, t: (b * tpb + t, 1)), kvspec(2), kvspec(3)],
        out_specs=pl.BlockSpec((tq, 512), lambda b, t: (b * tpb + t, 0)),
        out_shape=jax.ShapeDtypeStruct((n, 512), CDT),
        compiler_params=_params(("parallel", "parallel")), name="attn_b")(q2, kv2, kv2)


def _attn_c_kernel(q_ref, k_ref, v_ref, lam_ref, g_ref, o_ref, *, n_ctx, out_scale):
    lam = lam_ref[...]

    def run(nkeys):
        for h in range(C_HEADS):
            c0 = 2 * h * HEAD_DIM
            v = v_ref[0:nkeys, c0:c0 + 2 * HEAD_DIM]
            maps = []
            for t in range(2):
                cols = slice(c0 + t * HEAD_DIM, c0 + (t + 1) * HEAD_DIM)
                maps.append(_softmax_attend(q_ref[:, cols], k_ref[0:nkeys, cols], v))
            o = maps[0] - lam * maps[1]
            o = o * lax.rsqrt(jnp.mean(o * o, axis=-1, keepdims=True) + EPS) * g_ref[...] * out_scale
            o_ref[:, c0:c0 + 2 * HEAD_DIM] = o.astype(o_ref.dtype)

    _on_key_range(k_ref, n_ctx, run)


def _attn_c(q2, p, lam, subln_g, lam_init, batch, n_ctx):
    n = q2.shape[0]
    total = n // batch
    tq = TOK_TILE
    tpb = total // tq
    vec = pl.BlockSpec((1, 128), lambda b, t: (0, 0))
    return pl.pallas_call(
        functools.partial(_attn_c_kernel, n_ctx=n_ctx, out_scale=1.0 - lam_init), grid=(batch, tpb),
        in_specs=[pl.BlockSpec((tq, 512), lambda b, t: (b * tpb + t, 2)),
                  pl.BlockSpec((total, 512), lambda b, t: (b, 3)),
                  pl.BlockSpec((total, 512), lambda b, t: (b, _blk("c_v", 512))), vec, vec],
        out_specs=pl.BlockSpec((tq, 512), lambda b, t: (b * tpb + t, 0)),
        out_shape=jax.ShapeDtypeStruct((n, 512), CDT),
        compiler_params=_params(("parallel", "parallel")), name="attn_c")(
            q2, q2, p, jnp.broadcast_to(lam.astype(F32), (1, 128)), subln_g.astype(F32).reshape(1, 128))


def _silu(x):
    return x / (1.0 + jnp.exp(-x))


def _conv3(cur, prev_row, next_row, w_ref, b_ref):
    t = cur.shape[0]
    row = lax.broadcasted_iota(jnp.int32, cur.shape, 0)
    up = jnp.where(row == 0, prev_row, pltpu.roll(cur, 1, 0))
    dn = jnp.where(row == t - 1, next_row, pltpu.roll(cur, t - 1, 0))
    return up * w_ref[0:1, :] + cur * w_ref[1:2, :] + dn * w_ref[2:3, :] + b_ref[...]


def _ssd_prep_kernel(xs_ref, xsp_ref, xsn_ref, bc_ref, bcp_ref, bcn_ref, dt_ref, wx_ref, bx_ref, wbc_ref, bbc_ref,
                     dtb_ref, a_ref, xo_ref, bco_ref, dto_ref, ao_ref, *, tiles_per_batch):
    tb = pl.program_id(0) % tiles_per_batch
    has_prev = (tb >= 2).astype(F32)
    has_next = ((tb != 0) & (tb != tiles_per_batch - 1)).astype(F32)
    hl = xsp_ref.shape[0]
    xs = _conv3(xs_ref[...].astype(F32), xsp_ref[hl - 1:hl, :].astype(F32) * has_prev,
                xsn_ref[0:1, :].astype(F32) * has_next, wx_ref, bx_ref)
    xo_ref[...] = _silu(xs)
    bc = _conv3(bc_ref[...].astype(F32), bcp_ref[hl - 1:hl, :].astype(F32) * has_prev,
                bcn_ref[0:1, :].astype(F32) * has_next, wbc_ref, bbc_ref)
    bco_ref[...] = _silu(bc).astype(bco_ref.dtype)
    u = dt_ref[...] + dtb_ref[...]
    dt = jnp.maximum(u, 0.0) + jnp.log1p(jnp.exp(-jnp.abs(u)))
    dto_ref[...] = dt
    ao_ref[...] = dt * a_ref[...]


def _ssd_prep(p, dt_raw, conv_w, conv_b, dt_bias, a_log, tiles_per_batch):
    n = p.shape[0]
    tt = TOK_TILE
    halo = 16
    per = tt // halo
    last = n // halo - 1
    cur = lambda name, w: pl.BlockSpec((tt, w), lambda i: (i, _blk(name, w)))
    prv = lambda name, w: pl.BlockSpec((halo, w), lambda i: (jnp.maximum(i * per - 1, 0), _blk(name, w)))
    nxt = lambda name, w: pl.BlockSpec((halo, w), lambda i: (jnp.minimum((i + 1) * per, last), _blk(name, w)))
    full = lambda r, c: pl.BlockSpec((r, c), lambda i: (0, 0))
    pad = lambda v: jnp.pad(v.astype(F32).reshape(1, -1), ((0, 0), (0, DT_PAD - 2 * M_HEADS)))
    cw, cb = conv_w.astype(F32), conv_b.astype(F32).reshape(1, -1)
    row = lambda w: pl.BlockSpec((tt, w), lambda i: (i, 0))
    return pl.pallas_call(
        functools.partial(_ssd_prep_kernel, tiles_per_batch=tiles_per_batch), grid=(n // tt,),
        in_specs=[cur("d_xs", 1024), prv("d_xs", 1024), nxt("d_xs", 1024),
                  cur("d_b", 512), prv("d_b", 512), nxt("d_b", 512), row(DT_PAD),
                  full(3, 1024), full(1, 1024), full(3, 512), full(1, 512), full(1, DT_PAD), full(1, DT_PAD)],
        out_specs=[row(1024), row(512), row(DT_PAD), row(DT_PAD)],
        out_shape=[jax.ShapeDtypeStruct((n, 1024), F32), jax.ShapeDtypeStruct((n, 512), CDT),
                   jax.ShapeDtypeStruct((n, DT_PAD), F32), jax.ShapeDtypeStruct((n, DT_PAD), F32)],
        compiler_params=_params(("parallel",)), name="ssd_prep")(
            p, p, p, p, p, p, dt_raw, cw[:, :M_INNER], cb[:, :M_INNER], cw[:, M_INNER:], cb[:, M_INNER:],
            pad(dt_bias), pad(-jnp.exp(a_log.astype(F32))))


def _split3(x):
    hi = x.astype(jnp.bfloat16)
    r = x - hi.astype(F32)
    mid = r.astype(jnp.bfloat16)
    lo = (r - mid.astype(F32)).astype(jnp.bfloat16)
    return hi, mid, lo


def _expand_heads(v, first, count, rows):
    lane = lax.broadcasted_iota(jnp.int32, (rows, 2 * M_HEAD_DIM), 1)
    return jnp.concatenate([jnp.where(lane < M_HEAD_DIM, v[:, c:c + 1], v[:, c + 1:c + 2])
                            for c in range(first, first + count, 2)], axis=1)


def _ssd_kernel(*refs, rev):
    if rev:
        xs_ref, bc_ref, dt_ref, a_ref, yf_ref, z_ref, dsk_ref, ng_ref, o_ref, st_ref = refs
    else:
        xs_ref, bc_ref, dt_ref, a_ref, o_ref, st_ref = refs
    t = xs_ref.shape[0]

    @pl.when(pl.program_id(1) == 0)
    def _():
        st_ref[...] = jnp.zeros_like(st_ref)

    li = lax.broadcasted_iota(jnp.int32, (t, t), 0)
    si = lax.broadcasted_iota(jnp.int32, (t, t), 1)
    tri = (si >= li) if rev else (li >= si)
    tri_b = jnp.where(tri, 1.0, 0.0).astype(jnp.bfloat16)
    cs = sum(jnp.dot(tri_b, part, preferred_element_type=F32) for part in _split3(a_ref[...]))
    cst = cs.T
    tot = cs[0:1, :] if rev else cs[t - 1:t, :]
    outdec, indec, sdec = jnp.exp(cs), jnp.exp(tot - cs), jnp.exp(tot)
    dt = dt_ref[...]
    dtt = dt.T
    d0 = M_HEADS if rev else 0
    first_head = lax.broadcasted_iota(jnp.int32, (t, 2 * M_HEAD_DIM), 1) < M_HEAD_DIM
    ys = []
    for g in range(M_GROUPS):
        c0 = d0 + g * M_HPG
        lanes = slice(g * M_HPG * M_HEAD_DIM, (g + 1) * M_HPG * M_HEAD_DIM)
        bg = bc_ref[:, g * M_STATE:(g + 1) * M_STATE]
        cg = bc_ref[:, (M_GROUPS + g) * M_STATE:(M_GROUPS + g + 1) * M_STATE]
        cb = lax.dot_general(cg, bg, _NT, preferred_element_type=F32)
        bgt = bg.astype(F32).T.astype(bg.dtype)
        xs = xs_ref[:, lanes]
        xsb = xs.astype(bg.dtype)
        hst = st_ref[g]
        y_off = jnp.dot(cg, hst.astype(cg.dtype), preferred_element_type=F32) * _expand_heads(outdec, c0, M_HPG, t)
        y_diag = []
        for k in range(M_HPG // 2):
            xpair = xsb[:, 2 * k * M_HEAD_DIM:2 * (k + 1) * M_HEAD_DIM]
            halves = []
            for c in (c0 + 2 * k, c0 + 2 * k + 1):
                lm = jnp.exp(jnp.where(tri, cs[:, c:c + 1] - cst[c:c + 1, :], NEG)) * dtt[c:c + 1, :]
                halves.append(jnp.dot((cb * lm).astype(xsb.dtype), xpair, preferred_element_type=F32))
            y_diag.append(jnp.where(first_head, halves[0], halves[1]))
        ys.append(y_off + jnp.concatenate(y_diag, axis=1))
        xin = (xs * _expand_heads(dt * indec, c0, M_HPG, t)).astype(bg.dtype)
        st_ref[g] = hst * _expand_heads(sdec, c0, M_HPG, 1) + jnp.dot(bgt, xin, preferred_element_type=F32)
    y = jnp.concatenate(ys, axis=1)
    if not rev:
        o_ref[...] = y
        return
    y = (y + yf_ref[...] + xs_ref[...] * dsk_ref[...]) * _silu(z_ref[...].astype(F32))
    half = M_INNER // M_GROUPS
    parts = []
    for g in range(M_GROUPS):
        yg = y[:, g * half:(g + 1) * half]
        parts.append(yg * lax.rsqrt(jnp.mean(yg * yg, axis=-1, keepdims=True) + EPS))
    o_ref[...] = (jnp.concatenate(parts, axis=1) * ng_ref[...]).astype(o_ref.dtype)


def _ssd(xs, bc, dt, a, batch, n_ctx, rev, yf=None, p=None, d_skip=None, norm_g=None):
    n = xs.shape[0]
    t = CHUNK
    cpb = n // batch // t
    cc = n_ctx // t

    def chunk(b, c):
        if rev:
            c = jnp.where(c < cc, cc - 1 - c, cpb - 1 - (c - cc))
        return b * cpb + c

    row = lambda w: pl.BlockSpec((t, w), lambda b, c: (chunk(b, c), 0))
    in_specs = [row(1024), row(512), row(DT_PAD), row(DT_PAD)]
    args = [xs, bc, dt, a]
    if rev:
        vec = pl.BlockSpec((1, 1024), lambda b, c: (0, 0))
        in_specs += [row(1024), pl.BlockSpec((t, 1024), lambda b, c: (chunk(b, c), _blk("d_z", 1024))), vec, vec]
        args += [yf, p, jnp.repeat(d_skip.astype(F32), M_HEAD_DIM).reshape(1, -1), norm_g.astype(F32).reshape(1, -1)]
    return pl.pallas_call(
        functools.partial(_ssd_kernel, rev=rev), grid=(batch, cpb), in_specs=in_specs,
        out_specs=row(1024), out_shape=jax.ShapeDtypeStruct((n, 1024), CDT if rev else F32),
        scratch_shapes=[pltpu.VMEM((M_GROUPS, M_STATE, M_HPG * M_HEAD_DIM), F32)],
        compiler_params=_params(("parallel", "arbitrary")), name="ssd_rev" if rev else "ssd_fwd")(*args)


def _merge_kernel(oa_ref, ob_ref, oc_ref, od_ref, ga_ref, gb_ref, gc_ref, gd_ref,
                  wa_ref, wb_ref, wc_ref, wd_ref, wo_ref, o_ref):
    acc = None
    for o, g, w in ((oa_ref, ga_ref, wa_ref), (ob_ref, gb_ref, wb_ref), (oc_ref, gc_ref, wc_ref), (od_ref, gd_ref, wd_ref)):
        gate = 1.0 / (1.0 + jnp.exp(-g[...].astype(F32)))
        term = gate * jnp.dot(o[...], w[...], preferred_element_type=F32)
        acc = term if acc is None else acc + term
    o_ref[...] = jnp.dot(acc.astype(wo_ref.dtype), wo_ref[...], preferred_element_type=F32)


def _merge(oa, ob, oc, od, p, w_br, w_out):
    n = oa.shape[0]
    d = w_out.shape[0]
    tt = TOK_TILE
    row = lambda w: pl.BlockSpec((tt, w), lambda i: (i, 0))
    gate = lambda name: pl.BlockSpec((tt, d), lambda i: (i, _blk(name, d)))
    full = lambda r: pl.BlockSpec((r, d), lambda i: (0, 0))
    return pl.pallas_call(
        _merge_kernel, grid=(n // tt,),
        in_specs=[row(512), row(512), row(512), row(1024), gate("g_a"), gate("g_b"), gate("g_c"), gate("g_d"),
                  full(512), full(512), full(512), full(1024), full(d)],
        out_specs=row(d), out_shape=jax.ShapeDtypeStruct((n, d), F32),
        compiler_params=_params(("parallel",)), name="merge")(
            oa, ob, oc, od, p, p, p, p, w_br[0:512], w_br[512:1024], w_br[1024:1536], w_br[1536:2560], w_out)


def _silu_rows_kernel(c_ref, o_ref):
    o_ref[...] = _silu(c_ref[...]).astype(o_ref.dtype)


def _rope_tables(rows, n_ctx):
    row = jnp.repeat(jnp.arange(rows, dtype=F32), GRID_W)
    col = jnp.tile(jnp.arange(GRID_W, dtype=F32), rows)
    nq = HEAD_DIM // 4
    inv = ROPE_THETA ** (-jnp.arange(nq, dtype=F32) / nq)
    ar, ac = row[:, None] * inv, col[:, None] * inv
    ang = jnp.concatenate([ar, ar, ac, ac], axis=-1)
    ang = jnp.concatenate([jnp.zeros((n_ctx, HEAD_DIM), F32), ang], axis=0)
    return jnp.tile(jnp.cos(ang), (1, 2)), jnp.tile(jnp.sin(ang), (1, 2))


def kernel(x, c, ctx, c_ctx, w_ada, b_ada, g_norm1, g_norm2, w_in, a_sink, b_qnorm, b_knorm, c_lam_q1, c_lam_k1, c_lam_q2, c_lam_k2, c_subln, m_conv_w, m_conv_b, m_dt_bias, m_a_log, m_d, m_norm, w_br, w_out, p_wq, p_subkeys, p_u, p_v, g_final):
    batch, s, d = x.shape
    n_ctx = ctx.shape[1]
    depth = w_in.shape[0]
    assert n_ctx == TOK_TILE and s % TOK_TILE == 0 and s % GRID_W == 0
    tpb = (s + n_ctx) // TOK_TILE
    xc = jnp.concatenate([ctx, x], axis=1).reshape(-1, d)
    cos, sin = _rope_tables(s // GRID_W, n_ctx)

    cond = jnp.concatenate([c, c_ctx[None, :], jnp.zeros((16 - batch - 1, d), F32)], axis=0)
    cond = pl.pallas_call(_silu_rows_kernel, out_shape=jax.ShapeDtypeStruct(cond.shape, CDT), name="silu")(cond)

    def mod_rows(m):
        m6 = m.reshape(16, 6, d)
        lat, con = m6[:batch], jnp.broadcast_to(m6[batch][None], (batch, 6, d))
        both = jnp.stack([con, lat], axis=1).reshape(2 * batch, 6, d)
        return [both[:, k][:, None, :] for k in range(6)]

    w_proj = jnp.concatenate([w_in[:, :, _SRC[nm][0]:_SRC[nm][0] + _SRC[nm][1]] for nm in _ORDER], axis=-1).astype(CDT)
    o_dt = _SRC["d_dt"][0]
    w_dt = jnp.pad(w_in[:, :, o_dt:o_dt + 2 * M_HEADS], ((0, 0), (0, 0), (0, DT_PAD - 2 * M_HEADS))).astype(CDT)

    delta, gate_prev = None, None
    for l in range(depth):
        mods = _matmul(cond, w_ada[l].astype(CDT), F32, 16, 768, bias=b_ada[l].reshape(1, -1))
        sh1, sc1, gt1, sh2, sc2, gt2 = mod_rows(mods)
        xc, h = _norm_mod(xc, delta, gate_prev, g_norm1[l], sh1, sc1, tpb)
        tm = next(t for t in (8 * TOK_TILE, 2 * TOK_TILE, TOK_TILE) if h.shape[0] % t == 0)
        p = _matmul(h, w_proj[l], CDT, tm, P_WIDTH // 4)
        dt_raw = _matmul(h, w_dt[l], F32, tm, DT_PAD)
        q2, kv2 = _qk_prep(p, cos, sin, b_qnorm[l].astype(F32), b_knorm[l].astype(F32), tpb)
        o_a = _attn_a(q2, kv2, a_sink[l], batch, n_ctx)
        o_b = _attn_b(q2, kv2, batch, n_ctx)
        f32 = lambda v: v.astype(F32)
        lam_init = 0.8 - 0.6 * math.exp(-0.3 * l)
        lam = (jnp.exp(jnp.sum(f32(c_lam_q1[l]) * f32(c_lam_k1[l])))
               - jnp.exp(jnp.sum(f32(c_lam_q2[l]) * f32(c_lam_k2[l]))) + lam_init)
        o_c = _attn_c(q2, p, lam, c_subln[l], lam_init, batch, n_ctx)
        xs, bc, dt, a = _ssd_prep(p, dt_raw, m_conv_w[l], m_conv_b[l], m_dt_bias[l].reshape(-1), m_a_log[l].reshape(-1), tpb)
        y_f = _ssd(xs, bc, dt, a, batch, n_ctx, rev=False)
        o_d = _ssd(xs, bc, dt, a, batch, n_ctx, rev=True, yf=y_f, p=p, d_skip=m_d[l], norm_g=m_norm[l])
        mix = _merge(o_a, o_b, o_c, o_d, p, w_br[l].astype(CDT), w_out[l].astype(CDT))
        xc, _, ht = _norm_mod(xc, mix, gt1, g_norm2[l], sh2, sc2, tpb, want_t=True)
        delta = _peer(ht, p_wq[l].T.astype(CDT), p_subkeys[l].astype(CDT), p_u[l].astype(CDT), p_v[l].T.astype(CDT))
        gate_prev = gt2
    zero = jnp.zeros((2 * batch, 1, d), F32)
    _, y = _norm_mod(xc, delta, gate_prev, g_final, zero, zero, tpb, out_dtype=F32)
    return y.reshape(batch, n_ctx + s, d)[:, n_ctx:]
```

```python
import functools
import math

import jax
import jax.numpy as jnp
from jax import lax
from jax.experimental import pallas as pl
from jax.experimental.pallas import tpu as pltpu

F32 = jnp.float32
CDT = jnp.bfloat16

EPS = 1e-6
NEG = -1e30
HEAD_DIM = 64
GRID_W = 64
ROPE_THETA = 10000.0
WINDOW = 128
A_HEADS, A_KV = 8, 2
B_HEADS, B_KV = 8, 2
C_HEADS = 4
M_HEADS, M_HEAD_DIM, M_GROUPS, M_STATE = 16, 64, 2, 128
M_INNER = M_HEADS * M_HEAD_DIM
M_HPG = M_HEADS // M_GROUPS
CHUNK = 128
P_HEADS, N_KEYS, P_TOPK = 8, 128, 16
TOK_TILE = 256

VMEM_LIMIT = 48 * 1024 * 1024


def _params(sem):
    return pltpu.CompilerParams(dimension_semantics=sem, vmem_limit_bytes=VMEM_LIMIT)


def _mm_kernel(a_ref, b_ref, o_ref):
    o_ref[...] = jnp.dot(a_ref[...], b_ref[...], preferred_element_type=F32).astype(o_ref.dtype)


def _mm_bias_kernel(a_ref, b_ref, bias_ref, o_ref):
    acc = jnp.dot(a_ref[...], b_ref[...], preferred_element_type=F32)
    o_ref[...] = (acc + bias_ref[...]).astype(o_ref.dtype)


def _matmul(a, b, out_dtype, tm, tn, bias=None):
    m, k = a.shape
    n = b.shape[1]
    assert m % tm == 0 and n % tn == 0, (a.shape, b.shape, tm, tn)
    in_specs = [pl.BlockSpec((tm, k), lambda i, j: (i, 0)),
                pl.BlockSpec((k, tn), lambda i, j: (0, j))]
    args = [a, b]
    kern = _mm_kernel
    if bias is not None:
        in_specs.append(pl.BlockSpec((1, tn), lambda i, j: (0, j)))
        args.append(bias)
        kern = _mm_bias_kernel
    return pl.pallas_call(
        kern, grid=(m // tm, n // tn), in_specs=in_specs,
        out_specs=pl.BlockSpec((tm, tn), lambda i, j: (i, j)),
        out_shape=jax.ShapeDtypeStruct((m, n), out_dtype),
        compiler_params=_params(("parallel", "parallel")), name="matmul")(*args)


def _norm_mod_kernel(*refs, has_delta, want_t):
    it = iter(refs)
    x_ref = next(it)
    if has_delta:
        d_ref, gp_ref = next(it), next(it)
    g_ref, sh_ref, sc_ref = next(it), next(it), next(it)
    if has_delta:
        xo_ref = next(it)
    h_ref = next(it)
    x = x_ref[...]
    if has_delta:
        x = x + gp_ref[0] * d_ref[...]
        xo_ref[...] = x
    y = x * lax.rsqrt(jnp.mean(x * x, axis=-1, keepdims=True) + EPS) * g_ref[...]
    h = y * (1.0 + sc_ref[0]) + sh_ref[0]
    h_ref[...] = h.astype(h_ref.dtype)
    if want_t:
        ht_ref = next(it)
        ht_ref[...] = h.T.astype(ht_ref.dtype)


def _norm_mod(x, delta, gate_prev, g, shift, scale, tiles_per_batch, want_t=False, out_dtype=None):
    n, d = x.shape
    tt = TOK_TILE
    has_delta = delta is not None
    out_dtype = out_dtype or CDT

    def mod_idx(i):
        return ((i // tiles_per_batch) * 2 + jnp.minimum(i % tiles_per_batch, 1), 0, 0)

    row = pl.BlockSpec((tt, d), lambda i: (i, 0))
    mod = pl.BlockSpec((1, 1, d), mod_idx)
    in_specs, args = [row], [x]
    if has_delta:
        in_specs += [row, mod]
        args += [delta, gate_prev]
    in_specs += [pl.BlockSpec((1, d), lambda i: (0, 0)), mod, mod]
    args += [g.reshape(1, d), shift, scale]
    out_specs, out_shape = [], []
    if has_delta:
        out_specs.append(row)
        out_shape.append(jax.ShapeDtypeStruct((n, d), F32))
    out_specs.append(row)
    out_shape.append(jax.ShapeDtypeStruct((n, d), out_dtype))
    if want_t:
        out_specs.append(pl.BlockSpec((d, tt), lambda i: (0, i)))
        out_shape.append(jax.ShapeDtypeStruct((d, n), CDT))
    outs = pl.pallas_call(
        functools.partial(_norm_mod_kernel, has_delta=has_delta, want_t=want_t),
        grid=(n // tt,), in_specs=in_specs, out_specs=out_specs, out_shape=out_shape,
        compiler_params=_params(("parallel",)), name="norm_mod")(*args)
    outs = list(outs)
    x_new = outs.pop(0) if has_delta else x
    return (x_new, *outs)


_L2_SIZES = tuple(16 if a == 0 else 8 for a in range(P_TOPK))
_L2_VALID = tuple(P_TOPK // (a + 1) for a in range(P_TOPK))
_L2_ROWS = sum(_L2_SIZES)


def _knock_out_max(work, exact):
    m = jnp.max(work, axis=0, keepdims=True)
    hit = work == m
    if exact:
        row = lax.broadcasted_iota(jnp.int32, work.shape, 0)
        hit = row == jnp.min(jnp.where(hit, row, work.shape[0]), axis=0, keepdims=True)
    return m, hit


def _top16(s, exact):
    rank = jnp.full(s.shape, 127.0, F32)
    vals = []
    work = s
    for a in range(P_TOPK):
        m, hit = _knock_out_max(work, exact)
        rank = jnp.where(hit, float(a), rank)
        work = jnp.where(hit, -jnp.inf, work)
        vals.append(m)
    return jnp.concatenate(vals, axis=0), rank


def _route(s1, s2, exact):
    tt = s1.shape[1]
    sv1, rank1 = _top16(s1, exact)
    sv2, rank2 = _top16(s2, exact)
    pieces = []
    for a in range(P_TOPK):
        sz = _L2_SIZES[a]
        b_iota = lax.broadcasted_iota(jnp.int32, (sz, tt), 0)
        pieces.append(jnp.where(b_iota < _L2_VALID[a], sv1[a:a + 1, :] + sv2[0:sz, :], -jnp.inf))
    cand0 = jnp.concatenate(pieces, axis=0)
    best0 = sv1[0:1, :] + sv2[0:1, :]
    cand = cand0
    z = jnp.zeros((1, tt), F32)
    for _ in range(P_TOPK):
        m, hit = _knock_out_max(cand, exact)
        cand = jnp.where(hit, -jnp.inf, cand)
        z = z + jnp.exp(m - best0)
    sel = jnp.where(cand != cand0, 1.0, 0.0)
    cnt1 = jnp.zeros((N_KEYS, tt), F32)
    off = 0
    for a in range(P_TOPK):
        c_a = jnp.sum(sel[off:off + _L2_SIZES[a], :], axis=0, keepdims=True)
        cnt1 = jnp.where(rank1 == float(a), c_a, cnt1)
        off += _L2_SIZES[a]
    count = lambda r: jnp.sum(jnp.where(r < float(P_TOPK), 1.0, 0.0), axis=0, keepdims=True)
    excess = (count(rank1) + count(rank2) + jnp.sum(sel, axis=0, keepdims=True)) - 3.0 * P_TOPK
    return jnp.exp(s1 - sv1[0:1, :]), cnt1, jnp.exp(s2 - sv2[0:1, :]) / z, rank2, excess


def _route_kernel(qt_ref, sk_ref, e1_ref, c1_ref, e2_ref, r2_ref):
    s1 = jnp.dot(sk_ref[0, 0], qt_ref[0:N_KEYS, :], preferred_element_type=F32)
    s2 = jnp.dot(sk_ref[0, 1], qt_ref[N_KEYS:2 * N_KEYS, :], preferred_element_type=F32)

    def emit(exact):
        e1, cnt1, e2, rank2, excess = _route(s1, s2, exact)
        e1_ref[0] = e1
        c1_ref[0] = cnt1
        e2_ref[0] = e2.astype(e2_ref.dtype)
        r2_ref[0] = rank2.astype(r2_ref.dtype)
        return excess

    excess = emit(False)

    @pl.when(jnp.max(excess) > 0.0)
    def _():
        emit(True)


def _peer_route(qt, sub_keys, tt):
    n = qt.shape[1]
    dk = sub_keys.shape[-1]
    out = lambda dt: jax.ShapeDtypeStruct((P_HEADS, N_KEYS, n), dt)
    ospec = pl.BlockSpec((1, N_KEYS, tt), lambda i, h: (h, 0, i))
    return pl.pallas_call(
        _route_kernel, grid=(n // tt, P_HEADS),
        in_specs=[pl.BlockSpec((2 * dk, tt), lambda i, h: (h, i)),
                  pl.BlockSpec((1, 2, N_KEYS, dk), lambda i, h: (h, 0, 0, 0))],
        out_specs=[ospec] * 4, out_shape=[out(F32), out(F32), out(CDT), out(CDT)],
        compiler_params=_params(("parallel", "parallel")), name="peer_route")(qt, sub_keys)


def _gelu(x):
    return 0.5 * x * (1.0 + lax.erf(x * (2.0 ** -0.5)))


def _peer_dense_kernel(ht_ref, eu_ref, evt_ref, e1_ref, c1_ref, e2_ref, r2_ref, o_ref, acc_ref, gate_ref, *, rows):
    k = pl.program_id(1)

    @pl.when(k == 0)
    def _():
        acc_ref[...] = jnp.zeros_like(acc_ref)

    gdt = gate_ref.dtype
    tt = ht_ref.shape[1]
    zero = jnp.zeros((), gdt)

    def sublane_bcast(row_ref, h, i):
        tile = jnp.broadcast_to(row_ref[h, pl.ds(i, 1), :], (16, tt)).astype(gdt)
        return jnp.concatenate([tile] * (N_KEYS // 16), axis=0)

    for ii in range(rows):
        i = k * rows + ii
        g = None
        for h in range(P_HEADS):
            term = jnp.where(r2_ref[h] < sublane_bcast(c1_ref, h, i), e2_ref[h], zero) * sublane_bcast(e1_ref, h, i)
            g = term if g is None else g + term
        gate_ref[ii * N_KEYS:(ii + 1) * N_KEYS, :] = g

    parts = 4
    sub = rows * N_KEYS // parts
    ht = ht_ref[...]
    pre = [jnp.dot(eu_ref[j * sub:(j + 1) * sub, :], ht, preferred_element_type=F32) for j in range(parts)]
    for j in range(parts):
        cols = slice(j * sub, (j + 1) * sub)
        w = (gate_ref[cols, :] * _gelu(pre[j]).astype(gdt)).astype(evt_ref.dtype)
        acc_ref[...] += jnp.dot(evt_ref[:, cols], w, preferred_element_type=F32)

    @pl.when(k == pl.num_programs(1) - 1)
    def _():
        o_ref[...] = acc_ref[...].T


def _peer_dense(ht, eu, evt, e1, c1, e2, r2, tt, rows):
    d, n = ht.shape
    n_exp = eu.shape[0]
    eb = rows * N_KEYS
    rt = pl.BlockSpec((P_HEADS, N_KEYS, tt), lambda i, k: (0, 0, i))
    return pl.pallas_call(
        functools.partial(_peer_dense_kernel, rows=rows),
        grid=(n // tt, n_exp // eb),
        in_specs=[pl.BlockSpec((d, tt), lambda i, k: (0, i)),
                  pl.BlockSpec((eb, d), lambda i, k: (k, 0)),
                  pl.BlockSpec((d, eb), lambda i, k: (0, k)),
                  rt, rt, rt, rt],
        out_specs=pl.BlockSpec((tt, d), lambda i, k: (i, 0)),
        out_shape=jax.ShapeDtypeStruct((n, d), F32),
        scratch_shapes=[pltpu.VMEM((d, tt), F32), pltpu.VMEM((eb, tt), e2.dtype)],
        compiler_params=_params(("parallel", "arbitrary")), name="peer_dense")(ht, eu, evt, e1, c1, e2, r2)


def _peer(ht, wqt, sub_keys, eu, evt):
    n = ht.shape[1]
    tt = 512 if n % 512 == 0 else 256
    qt = _matmul(wqt, ht, CDT, wqt.shape[0], tt)
    e1, c1, e2, r2 = _peer_route(qt, sub_keys, 256)
    return _peer_dense(ht, eu, evt, e1, c1, e2, r2, tt, rows=16)


_SRC = {}
_off = 0
for _name, _w in (("a_q", 512), ("a_k", 128), ("a_v", 128), ("b_q", 512), ("b_k", 128), ("b_v", 128),
                  ("c_q", 512), ("c_k", 512), ("c_v", 512), ("d_z", 1024), ("d_xs", 1024), ("d_b", 256),
                  ("d_c", 256), ("d_dt", 32), ("g_a", 1024), ("g_b", 1024), ("g_c", 1024), ("g_d", 1024)):
    _SRC[_name] = (_off, _w)
    _off += _w
_ORDER = ("d_z", "d_xs", "g_a", "g_b", "g_c", "g_d", "a_q", "b_q", "c_q", "c_k", "c_v",
          "d_b", "d_c", "a_k", "a_v", "b_k", "b_v")
_DST = {}
_off = 0
for _name in _ORDER:
    _DST[_name] = _off
    _off += _SRC[_name][1]
P_WIDTH = _off
DT_PAD = 128


def _blk(name, width):
    return _DST[name] // width


def _rope(x, cos, sin):
    w = x.shape[1]
    lane = lax.broadcasted_iota(jnp.int32, x.shape, 1)
    fwd = pltpu.roll(x, 16, 1)
    bwd = pltpu.roll(x, w - 16, 1)
    rot = jnp.where(lane % 32 < 16, -bwd, fwd)
    return x * cos + rot * sin


def _head_rms(x, bd):
    sq = x * x
    hi = sq.astype(jnp.bfloat16)
    lo = (sq - hi.astype(F32)).astype(jnp.bfloat16)
    ms = jnp.dot(hi, bd, preferred_element_type=F32) + jnp.dot(lo, bd, preferred_element_type=F32)
    return lax.rsqrt(ms + EPS)


def _qk_prep_kernel(q_ref, kv_ref, cs_ref, sn_ref, qg_ref, kg_ref, bd_ref, qo_ref, kvo_ref):
    cos2, sin2 = cs_ref[...], sn_ref[...]
    cos4 = jnp.concatenate([cos2] * 4, axis=1)
    sin4 = jnp.concatenate([sin2] * 4, axis=1)
    scale = HEAD_DIM ** -0.5
    bd = bd_ref[...]
    a_q = q_ref[:, 0:512].astype(F32)
    qo_ref[:, 0:512] = (_rope(a_q, cos4, sin4) * scale).astype(qo_ref.dtype)
    b_q = q_ref[:, 512:1024].astype(F32)
    b_q = b_q * _head_rms(b_q, bd) * qg_ref[...]
    qo_ref[:, 512:1024] = (_rope(b_q, cos4, sin4) * scale).astype(qo_ref.dtype)
    c_q = q_ref[:, 1024:1536].astype(F32)
    qo_ref[:, 1024:1536] = (_rope(c_q, cos4, sin4) * scale).astype(qo_ref.dtype)
    c_k = q_ref[:, 1536:2048].astype(F32)
    qo_ref[:, 1536:2048] = _rope(c_k, cos4, sin4).astype(qo_ref.dtype)
    a_k = kv_ref[:, 0:128].astype(F32)
    kvo_ref[:, 0:128] = _rope(a_k, cos2, sin2).astype(kvo_ref.dtype)
    kvo_ref[:, 128:256] = kv_ref[:, 128:256]
    b_k = kv_ref[:, 256:384].astype(F32)
    b_k = b_k * _head_rms(b_k, bd[0:128, 0:128]) * kg_ref[...]
    kvo_ref[:, 256:384] = _rope(b_k, cos2, sin2).astype(kvo_ref.dtype)
    kvo_ref[:, 384:512] = kv_ref[:, 384:512]


def _qk_prep(p, cos, sin, qn_g, kn_g, tiles_per_batch):
    n = p.shape[0]
    tt = TOK_TILE
    lane = jnp.arange(512)
    bd = jnp.where(lane[:, None] // HEAD_DIM == lane[None, :] // HEAD_DIM, 1.0 / HEAD_DIM, 0.0).astype(jnp.bfloat16)
    pos = pl.BlockSpec((tt, 128), lambda i: (i % tiles_per_batch, 0))
    full = lambda r, c: pl.BlockSpec((r, c), lambda i: (0, 0))
    return pl.pallas_call(
        _qk_prep_kernel, grid=(n // tt,),
        in_specs=[pl.BlockSpec((tt, 2048), lambda i: (i, _blk("a_q", 2048))),
                  pl.BlockSpec((tt, 512), lambda i: (i, _blk("a_k", 512))),
                  pos, pos, full(1, 512), full(1, 128), full(512, 512)],
        out_specs=[pl.BlockSpec((tt, 2048), lambda i: (i, 0)), pl.BlockSpec((tt, 512), lambda i: (i, 0))],
        out_shape=[jax.ShapeDtypeStruct((n, 2048), CDT), jax.ShapeDtypeStruct((n, 512), CDT)],
        compiler_params=_params(("parallel",)), name="qk_prep")(
            p, p, cos, sin, jnp.tile(qn_g, 8).reshape(1, 512), jnp.tile(kn_g, 2).reshape(1, 128), bd)


_NT = (((1,), (1,)), ((), ()))


def _stack_heads(ref, first, count):
    return jnp.concatenate([ref[:, (first + i) * HEAD_DIM:(first + i + 1) * HEAD_DIM] for i in range(count)], axis=0)


def _attn_a_kernel(q_ref, k_ref, v_ref, sink_ref, o_ref, *, n_ctx):
    tq = q_ref.shape[0]
    total = k_ref.shape[0]
    t = pl.program_id(1)
    n = t - n_ctx // tq
    kstart = pl.multiple_of(jnp.clip(n_ctx + (n - 1) * tq, n_ctx - tq, total - 3 * tq), tq)
    col = lax.broadcasted_iota(jnp.int32, (tq, n_ctx + 3 * tq), 1)
    qpos = n * tq + lax.broadcasted_iota(jnp.int32, (tq, n_ctx + 3 * tq), 0)
    kpos = kstart - 2 * n_ctx + col
    valid = (col < n_ctx) | ((jnp.abs(qpos - kpos) <= WINDOW) & (kpos >= 0) & (n >= 0))
    groups = A_HEADS // A_KV
    for g in range(A_KV):
        q = _stack_heads(q_ref, g * groups, groups)
        hs = slice(g * HEAD_DIM, (g + 1) * HEAD_DIM)
        k = jnp.concatenate([k_ref[0:n_ctx, hs], k_ref[pl.ds(kstart, 3 * tq), hs]], axis=0)
        v = jnp.concatenate([v_ref[0:n_ctx, hs], v_ref[pl.ds(kstart, 3 * tq), hs]], axis=0)
        s = lax.dot_general(q, k, _NT, preferred_element_type=F32)
        for i in range(groups):
            hq = g * groups + i
            sink = sink_ref[hq:hq + 1, 0:1]
            si = jnp.where(valid, s[i * tq:(i + 1) * tq], NEG)
            m = jnp.maximum(jnp.max(si, axis=1, keepdims=True), sink)
            e = jnp.exp(si - m)
            den = jnp.sum(e, axis=1, keepdims=True) + jnp.exp(sink - m)
            o = jnp.dot(e.astype(v.dtype), v, preferred_element_type=F32)
            o_ref[:, hq * HEAD_DIM:(hq + 1) * HEAD_DIM] = (o / den).astype(o_ref.dtype)


def _attn_a(q2, kv2, sink, batch, n_ctx):
    n = q2.shape[0]
    total = n // batch
    tq = WINDOW
    tpb = total // tq
    kvspec = lambda c: pl.BlockSpec((total, 128), lambda b, t: (b, c))
    return pl.pallas_call(
        functools.partial(_attn_a_kernel, n_ctx=n_ctx), grid=(batch, tpb),
        in_specs=[pl.BlockSpec((tq, 512), lambda b, t: (b * tpb + t, 0)), kvspec(0), kvspec(1),
                  pl.BlockSpec((A_HEADS, 128), lambda b, t: (0, 0))],
        out_specs=pl.BlockSpec((tq, 512), lambda b, t: (b * tpb + t, 0)),
        out_shape=jax.ShapeDtypeStruct((n, 512), CDT),
        compiler_params=_params(("parallel", "parallel")), name="attn_a")(
            q2, kv2, kv2, jnp.broadcast_to(sink.astype(F32)[:, None], (A_HEADS, 128)))


def _softmax_attend(q, k, v):
    s = lax.dot_general(q, k, _NT, preferred_element_type=F32)
    p = jnp.exp(s - jnp.max(s, axis=1, keepdims=True))
    l = jnp.sum(p, axis=1, keepdims=True)
    return jnp.dot(p.astype(v.dtype), v, preferred_element_type=F32) / l


def _on_key_range(k_ref, n_ctx, run):
    @pl.when(pl.program_id(1) == 0)
    def _():
        run(n_ctx)

    @pl.when(pl.program_id(1) != 0)
    def _():
        run(k_ref.shape[0])


def _attn_b_kernel(q_ref, k_ref, v_ref, o_ref, *, n_ctx):
    tq = q_ref.shape[0]
    groups = B_HEADS // B_KV
    stack = 1

    def run(nkeys):
        for g in range(B_KV):
            hs = slice(g * HEAD_DIM, (g + 1) * HEAD_DIM)
            k, v = k_ref[0:nkeys, hs], v_ref[0:nkeys, hs]
            for j in range(groups // stack):
                first = g * groups + j * stack
                o = _softmax_attend(_stack_heads(q_ref, first, stack), k, v)
                for i in range(stack):
                    hq = first + i
                    o_ref[:, hq * HEAD_DIM:(hq + 1) * HEAD_DIM] = o[i * tq:(i + 1) * tq].astype(o_ref.dtype)

    _on_key_range(k_ref, n_ctx, run)


def _attn_b(q2, kv2, batch, n_ctx):
    n = q2.shape[0]
    total = n // batch
    tq = TOK_TILE
    tpb = total // tq
    kvspec = lambda c: pl.BlockSpec((total, 128), lambda b, t: (b, c))
    return pl.pallas_call(
        functools.partial(_attn_b_kernel, n_ctx=n_ctx), grid=(batch, tpb),
        in_specs=[pl.BlockSpec((tq, 512), lambda b, t: (b * tpb + t, 1)), kvspec(2), kvspec(3)],
        out_specs=pl.BlockSpec((tq, 512), lambda b, t: (b * tpb + t, 0)),
        out_shape=jax.ShapeDtypeStruct((n, 512), CDT),
        compiler_params=_params(("parallel", "parallel")), name="attn_b")(q2, kv2, kv2)


def _attn_c_kernel(q_ref, k_ref, v_ref, lam_ref, g_ref, o_ref, *, n_ctx, out_scale):
    lam = lam_ref[...]

    def run(nkeys):
        for h in range(C_HEADS):
            c0 = 2 * h * HEAD_DIM
            v = v_ref[0:nkeys, c0:c0 + 2 * HEAD_DIM]
            maps = []
            for t in range(2):
                cols = slice(c0 + t * HEAD_DIM, c0 + (t + 1) * HEAD_DIM)
                maps.append(_softmax_attend(q_ref[:, cols], k_ref[0:nkeys, cols], v))
            o = maps[0] - lam * maps[1]
            o = o * lax.rsqrt(jnp.mean(o * o, axis=-1, keepdims=True) + EPS) * g_ref[...] * out_scale
            o_ref[:, c0:c0 + 2 * HEAD_DIM] = o.astype(o_ref.dtype)

    _on_key_range(k_ref, n_ctx, run)


def _attn_c(q2, p, lam, subln_g, lam_init, batch, n_ctx):
    n = q2.shape[0]
    total = n // batch
    tq = TOK_TILE
    tpb = total // tq
    vec = pl.BlockSpec((1, 128), lambda b, t: (0, 0))
    return pl.pallas_call(
        functools.partial(_attn_c_kernel, n_ctx=n_ctx, out_scale=1.0 - lam_init), grid=(batch, tpb),
        in_specs=[pl.BlockSpec((tq, 512), lambda b, t: (b * tpb + t, 2)),
                  pl.BlockSpec((total, 512), lambda b, t: (b, 3)),
                  pl.BlockSpec((total, 512), lambda b, t: (b, _blk("c_v", 512))), vec, vec],
        out_specs=pl.BlockSpec((tq, 512), lambda b, t: (b * tpb + t, 0)),
        out_shape=jax.ShapeDtypeStruct((n, 512), CDT),
        compiler_params=_params(("parallel", "parallel")), name="attn_c")(
            q2, q2, p, jnp.broadcast_to(lam.astype(F32), (1, 128)), subln_g.astype(F32).reshape(1, 128))


def _silu(x):
    return x / (1.0 + jnp.exp(-x))


def _conv3(cur, prev_row, next_row, w_ref, b_ref):
    t = cur.shape[0]
    row = lax.broadcasted_iota(jnp.int32, cur.shape, 0)
    up = jnp.where(row == 0, prev_row, pltpu.roll(cur, 1, 0))
    dn = jnp.where(row == t - 1, next_row, pltpu.roll(cur, t - 1, 0))
    return up * w_ref[0:1, :] + cur * w_ref[1:2, :] + dn * w_ref[2:3, :] + b_ref[...]


def _ssd_prep_kernel(xs_ref, xsp_ref, xsn_ref, bc_ref, bcp_ref, bcn_ref, dt_ref, wx_ref, bx_ref, wbc_ref, bbc_ref,
                     dtb_ref, a_ref, xo_ref, bco_ref, dto_ref, ao_ref, *, tiles_per_batch):
    tb = pl.program_id(0) % tiles_per_batch
    has_prev = (tb >= 2).astype(F32)
    has_next = ((tb != 0) & (tb != tiles_per_batch - 1)).astype(F32)
    hl = xsp_ref.shape[0]
    xs = _conv3(xs_ref[...].astype(F32), xsp_ref[hl - 1:hl, :].astype(F32) * has_prev,
                xsn_ref[0:1, :].astype(F32) * has_next, wx_ref, bx_ref)
    xo_ref[...] = _silu(xs)
    bc = _conv3(bc_ref[...].astype(F32), bcp_ref[hl - 1:hl, :].astype(F32) * has_prev,
                bcn_ref[0:1, :].astype(F32) * has_next, wbc_ref, bbc_ref)
    bco_ref[...] = _silu(bc).astype(bco_ref.dtype)
    u = dt_ref[...] + dtb_ref[...]
    dt = jnp.maximum(u, 0.0) + jnp.log1p(jnp.exp(-jnp.abs(u)))
    dto_ref[...] = dt
    ao_ref[...] = dt * a_ref[...]


def _ssd_prep(p, dt_raw, conv_w, conv_b, dt_bias, a_log, tiles_per_batch):
    n = p.shape[0]
    tt = TOK_TILE
    halo = 16
    per = tt // halo
    last = n // halo - 1
    cur = lambda name, w: pl.BlockSpec((tt, w), lambda i: (i, _blk(name, w)))
    prv = lambda name, w: pl.BlockSpec((halo, w), lambda i: (jnp.maximum(i * per - 1, 0), _blk(name, w)))
    nxt = lambda name, w: pl.BlockSpec((halo, w), lambda i: (jnp.minimum((i + 1) * per, last), _blk(name, w)))
    full = lambda r, c: pl.BlockSpec((r, c), lambda i: (0, 0))
    pad = lambda v: jnp.pad(v.astype(F32).reshape(1, -1), ((0, 0), (0, DT_PAD - 2 * M_HEADS)))
    cw, cb = conv_w.astype(F32), conv_b.astype(F32).reshape(1, -1)
    row = lambda w: pl.BlockSpec((tt, w), lambda i: (i, 0))
    return pl.pallas_call(
        functools.partial(_ssd_prep_kernel, tiles_per_batch=tiles_per_batch), grid=(n // tt,),
        in_specs=[cur("d_xs", 1024), prv("d_xs", 1024), nxt("d_xs", 1024),
                  cur("d_b", 512), prv("d_b", 512), nxt("d_b", 512), row(DT_PAD),
                  full(3, 1024), full(1, 1024), full(3, 512), full(1, 512), full(1, DT_PAD), full(1, DT_PAD)],
        out_specs=[row(1024), row(512), row(DT_PAD), row(DT_PAD)],
        out_shape=[jax.ShapeDtypeStruct((n, 1024), F32), jax.ShapeDtypeStruct((n, 512), CDT),
                   jax.ShapeDtypeStruct((n, DT_PAD), F32), jax.ShapeDtypeStruct((n, DT_PAD), F32)],
        compiler_params=_params(("parallel",)), name="ssd_prep")(
            p, p, p, p, p, p, dt_raw, cw[:, :M_INNER], cb[:, :M_INNER], cw[:, M_INNER:], cb[:, M_INNER:],
            pad(dt_bias), pad(-jnp.exp(a_log.astype(F32))))


def _split3(x):
    hi = x.astype(jnp.bfloat16)
    r = x - hi.astype(F32)
    mid = r.astype(jnp.bfloat16)
    lo = (r - mid.astype(F32)).astype(jnp.bfloat16)
    return hi, mid, lo


def _expand_heads(v, first, count, rows):
    lane = lax.broadcasted_iota(jnp.int32, (rows, 2 * M_HEAD_DIM), 1)
    return jnp.concatenate([jnp.where(lane < M_HEAD_DIM, v[:, c:c + 1], v[:, c + 1:c + 2])
                            for c in range(first, first + count, 2)], axis=1)


def _ssd_kernel(*refs, rev):
    if rev:
        xs_ref, bc_ref, dt_ref, a_ref, yf_ref, z_ref, dsk_ref, ng_ref, o_ref, st_ref = refs
    else:
        xs_ref, bc_ref, dt_ref, a_ref, o_ref, st_ref = refs
    t = xs_ref.shape[0]

    @pl.when(pl.program_id(1) == 0)
    def _():
        st_ref[...] = jnp.zeros_like(st_ref)

    li = lax.broadcasted_iota(jnp.int32, (t, t), 0)
    si = lax.broadcasted_iota(jnp.int32, (t, t), 1)
    tri = (si >= li) if rev else (li >= si)
    tri_b = jnp.where(tri, 1.0, 0.0).astype(jnp.bfloat16)
    cs = sum(jnp.dot(tri_b, part, preferred_element_type=F32) for part in _split3(a_ref[...]))
    cst = cs.T
    tot = cs[0:1, :] if rev else cs[t - 1:t, :]
    outdec, indec, sdec = jnp.exp(cs), jnp.exp(tot - cs), jnp.exp(tot)
    dt = dt_ref[...]
    dtt = dt.T
    d0 = M_HEADS if rev else 0
    first_head = lax.broadcasted_iota(jnp.int32, (t, 2 * M_HEAD_DIM), 1) < M_HEAD_DIM
    ys = []
    for g in range(M_GROUPS):
        c0 = d0 + g * M_HPG
        lanes = slice(g * M_HPG * M_HEAD_DIM, (g + 1) * M_HPG * M_HEAD_DIM)
        bg = bc_ref[:, g * M_STATE:(g + 1) * M_STATE]
        cg = bc_ref[:, (M_GROUPS + g) * M_STATE:(M_GROUPS + g + 1) * M_STATE]
        cb = lax.dot_general(cg, bg, _NT, preferred_element_type=F32)
        bgt = bg.astype(F32).T.astype(bg.dtype)
        xs = xs_ref[:, lanes]
        xsb = xs.astype(bg.dtype)
        hst = st_ref[g]
        y_off = jnp.dot(cg, hst.astype(cg.dtype), preferred_element_type=F32) * _expand_heads(outdec, c0, M_HPG, t)
        y_diag = []
        for k in range(M_HPG // 2):
            xpair = xsb[:, 2 * k * M_HEAD_DIM:2 * (k + 1) * M_HEAD_DIM]
            halves = []
            for c in (c0 + 2 * k, c0 + 2 * k + 1):
                lm = jnp.exp(jnp.where(tri, cs[:, c:c + 1] - cst[c:c + 1, :], NEG)) * dtt[c:c + 1, :]
                halves.append(jnp.dot((cb * lm).astype(xsb.dtype), xpair, preferred_element_type=F32))
            y_diag.append(jnp.where(first_head, halves[0], halves[1]))
        ys.append(y_off + jnp.concatenate(y_diag, axis=1))
        xin = (xs * _expand_heads(dt * indec, c0, M_HPG, t)).astype(bg.dtype)
        st_ref[g] = hst * _expand_heads(sdec, c0, M_HPG, 1) + jnp.dot(bgt, xin, preferred_element_type=F32)
    y = jnp.concatenate(ys, axis=1)
    if not rev:
        o_ref[...] = y
        return
    y = (y + yf_ref[...] + xs_ref[...] * dsk_ref[...]) * _silu(z_ref[...].astype(F32))
    half = M_INNER // M_GROUPS
    parts = []
    for g in range(M_GROUPS):
        yg = y[:, g * half:(g + 1) * half]
        parts.append(yg * lax.rsqrt(jnp.mean(yg * yg, axis=-1, keepdims=True) + EPS))
    o_ref[...] = (jnp.concatenate(parts, axis=1) * ng_ref[...]).astype(o_ref.dtype)


def _ssd(xs, bc, dt, a, batch, n_ctx, rev, yf=None, p=None, d_skip=None, norm_g=None):
    n = xs.shape[0]
    t = CHUNK
    cpb = n // batch // t
    cc = n_ctx // t

    def chunk(b, c):
        if rev:
            c = jnp.where(c < cc, cc - 1 - c, cpb - 1 - (c - cc))
        return b * cpb + c

    row = lambda w: pl.BlockSpec((t, w), lambda b, c: (chunk(b, c), 0))
    in_specs = [row(1024), row(512), row(DT_PAD), row(DT_PAD)]
    args = [xs, bc, dt, a]
    if rev:
        vec = pl.BlockSpec((1, 1024), lambda b, c: (0, 0))
        in_specs += [row(1024), pl.BlockSpec((t, 1024), lambda b, c: (chunk(b, c), _blk("d_z", 1024))), vec, vec]
        args += [yf, p, jnp.repeat(d_skip.astype(F32), M_HEAD_DIM).reshape(1, -1), norm_g.astype(F32).reshape(1, -1)]
    return pl.pallas_call(
        functools.partial(_ssd_kernel, rev=rev), grid=(batch, cpb), in_specs=in_specs,
        out_specs=row(1024), out_shape=jax.ShapeDtypeStruct((n, 1024), CDT if rev else F32),
        scratch_shapes=[pltpu.VMEM((M_GROUPS, M_STATE, M_HPG * M_HEAD_DIM), F32)],
        compiler_params=_params(("parallel", "arbitrary")), name="ssd_rev" if rev else "ssd_fwd")(*args)


def _merge_kernel(oa_ref, ob_ref, oc_ref, od_ref, ga_ref, gb_ref, gc_ref, gd_ref,
                  wa_ref, wb_ref, wc_ref, wd_ref, wo_ref, o_ref):
    acc = None
    for o, g, w in ((oa_ref, ga_ref, wa_ref), (ob_ref, gb_ref, wb_ref), (oc_ref, gc_ref, wc_ref), (od_ref, gd_ref, wd_ref)):
        gate = 1.0 / (1.0 + jnp.exp(-g[...].astype(F32)))
        term = gate * jnp.dot(o[...], w[...], preferred_element_type=F32)
        acc = term if acc is None else acc + term
    o_ref[...] = jnp.dot(acc.astype(wo_ref.dtype), wo_ref[...], preferred_element_type=F32)


def _merge(oa, ob, oc, od, p, w_br, w_out):
    n = oa.shape[0]
    d = w_out.shape[0]
    tt = TOK_TILE
    row = lambda w: pl.BlockSpec((tt, w), lambda i: (i, 0))
    gate = lambda name: pl.BlockSpec((tt, d), lambda i: (i, _blk(name, d)))
    full = lambda r: pl.BlockSpec((r, d), lambda i: (0, 0))
    return pl.pallas_call(
        _merge_kernel, grid=(n // tt,),
        in_specs=[row(512), row(512), row(512), row(1024), gate("g_a"), gate("g_b"), gate("g_c"), gate("g_d"),
                  full(512), full(512), full(512), full(1024), full(d)],
        out_specs=row(d), out_shape=jax.ShapeDtypeStruct((n, d), F32),
        compiler_params=_params(("parallel",)), name="merge")(
            oa, ob, oc, od, p, p, p, p, w_br[0:512], w_br[512:1024], w_br[1024:1536], w_br[1536:2560], w_out)


def _silu_rows_kernel(c_ref, o_ref):
    o_ref[...] = _silu(c_ref[...]).astype(o_ref.dtype)


def _rope_tables(rows, n_ctx):
    row = jnp.repeat(jnp.arange(rows, dtype=F32), GRID_W)
    col = jnp.tile(jnp.arange(GRID_W, dtype=F32), rows)
    nq = HEAD_DIM // 4
    inv = ROPE_THETA ** (-jnp.arange(nq, dtype=F32) / nq)
    ar, ac = row[:, None] * inv, col[:, None] * inv
    ang = jnp.concatenate([ar, ar, ac, ac], axis=-1)
    ang = jnp.concatenate([jnp.zeros((n_ctx, HEAD_DIM), F32), ang], axis=0)
    return jnp.tile(jnp.cos(ang), (1, 2)), jnp.tile(jnp.sin(ang), (1, 2))


def kernel(x, c, ctx, c_ctx, w_ada, b_ada, g_norm1, g_norm2, w_in, a_sink, b_qnorm, b_knorm, c_lam_q1, c_lam_k1, c_lam_q2, c_lam_k2, c_subln, m_conv_w, m_conv_b, m_dt_bias, m_a_log, m_d, m_norm, w_br, w_out, p_wq, p_subkeys, p_u, p_v, g_final):
    batch, s, d = x.shape
    n_ctx = ctx.shape[1]
    depth = w_in.shape[0]
    assert n_ctx == TOK_TILE and s % TOK_TILE == 0 and s % GRID_W == 0
    tpb = (s + n_ctx) // TOK_TILE
    xc = jnp.concatenate([ctx, x], axis=1).reshape(-1, d)
    cos, sin = _rope_tables(s // GRID_W, n_ctx)

    cond = jnp.concatenate([c, c_ctx[None, :], jnp.zeros((16 - batch - 1, d), F32)], axis=0)
    cond = pl.pallas_call(_silu_rows_kernel, out_shape=jax.ShapeDtypeStruct(cond.shape, CDT), name="silu")(cond)

    def mod_rows(m):
        m6 = m.reshape(16, 6, d)
        lat, con = m6[:batch], jnp.broadcast_to(m6[batch][None], (batch, 6, d))
        both = jnp.stack([con, lat], axis=1).reshape(2 * batch, 6, d)
        return [both[:, k][:, None, :] for k in range(6)]

    w_proj = jnp.concatenate([w_in[:, :, _SRC[nm][0]:_SRC[nm][0] + _SRC[nm][1]] for nm in _ORDER], axis=-1).astype(CDT)
    o_dt = _SRC["d_dt"][0]
    w_dt = jnp.pad(w_in[:, :, o_dt:o_dt + 2 * M_HEADS], ((0, 0), (0, 0), (0, DT_PAD - 2 * M_HEADS))).astype(CDT)

    delta, gate_prev = None, None
    for l in range(depth):
        mods = _matmul(cond, w_ada[l].astype(CDT), F32, 16, 768, bias=b_ada[l].reshape(1, -1))
        sh1, sc1, gt1, sh2, sc2, gt2 = mod_rows(mods)
        xc, h = _norm_mod(xc, delta, gate_prev, g_norm1[l], sh1, sc1, tpb)
        tm = next(t for t in (8 * TOK_TILE, 2 * TOK_TILE, TOK_TILE) if h.shape[0] % t == 0)
        p = _matmul(h, w_proj[l], CDT, tm, P_WIDTH // 4)
        dt_raw = _matmul(h, w_dt[l], F32, tm, DT_PAD)
        q2, kv2 = _qk_prep(p, cos, sin, b_qnorm[l].astype(F32), b_knorm[l].astype(F32), tpb)
        o_a = _attn_a(q2, kv2, a_sink[l], batch, n_ctx)
        o_b = _attn_b(q2, kv2, batch, n_ctx)
        f32 = lambda v: v.astype(F32)
        lam_init = 0.8 - 0.6 * math.exp(-0.3 * l)
        lam = (jnp.exp(jnp.sum(f32(c_lam_q1[l]) * f32(c_lam_k1[l])))
               - jnp.exp(jnp.sum(f32(c_lam_q2[l]) * f32(c_lam_k2[l]))) + lam_init)
        o_c = _attn_c(q2, p, lam, c_subln[l], lam_init, batch, n_ctx)
        xs, bc, dt, a = _ssd_prep(p, dt_raw, m_conv_w[l], m_conv_b[l], m_dt_bias[l].reshape(-1), m_a_log[l].reshape(-1), tpb)
        y_f = _ssd(xs, bc, dt, a, batch, n_ctx, rev=False)
        o_d = _ssd(xs, bc, dt, a, batch, n_ctx, rev=True, yf=y_f, p=p, d_skip=m_d[l], norm_g=m_norm[l])
        mix = _merge(o_a, o_b, o_c, o_d, p, w_br[l].astype(CDT), w_out[l].astype(CDT))
        xc, _, ht = _norm_mod(xc, mix, gt1, g_norm2[l], sh2, sc2, tpb, want_t=True)
        delta = _peer(ht, p_wq[l].T.astype(CDT), p_subkeys[l].astype(CDT), p_u[l].astype(CDT), p_v[l].T.astype(CDT))
        gate_prev = gt2
    zero = jnp.zeros((2 * batch, 1, d), F32)
    _, y = _norm_mod(xc, delta, gate_prev, g_final, zero, zero, tpb, out_dtype=F32)
    return y.reshape(batch, n_ctx + s, d)[:, n_ctx:]
```
